```python
import jax, jax.numpy as jnp
from jax import lax
import numpy as np

D_MODEL = 2048
BATCH = 2
SEQ = 4096
DEPTH = 2
DEC_BATCH = 32
DEC_SEQ = 64
PAST_LEN = 1024

CHUNK = 64
CONV_W = 31
N_HEADS = 32
N_KV = 4
HEAD_DIM = 64
GROUP = N_HEADS // N_KV
WINDOW = 128
WIN_CHUNKS = WINDOW // CHUNK
BAND = (WIN_CHUNKS + 1) * CHUNK
D_FF = -(-8 * D_MODEL // (3 * 256)) * 256
N_A = (DEPTH + 1) // 2
N_B = DEPTH // 2
QKV_COLS = (N_HEADS + 2 * N_KV) * HEAD_DIM
EPS = 1e-6
NEG = -1e30

kernel_name = "hybrid_conv_swa_sink_alibi_stream_step"


def rms_norm(x, g):
    xf = x.astype(jnp.float32)
    y = xf * lax.rsqrt(jnp.mean(xf * xf, axis=-1, keepdims=True) + EPS)
    return (y * g.astype(jnp.float32)).astype(x.dtype)


def layer_norm(x, g, b):
    xf = x.astype(jnp.float32)
    mu = jnp.mean(xf, axis=-1, keepdims=True)
    var = jnp.mean(jnp.square(xf - mu), axis=-1, keepdims=True)
    y = (xf - mu) * lax.rsqrt(var + EPS) * g.astype(jnp.float32) + b.astype(jnp.float32)
    return y.astype(x.dtype)


def modulate(h, shift, scale):
    return h * (1 + scale[:, None, :]) + shift[:, None, :]


def alibi_slopes():
    h = jnp.arange(1, N_HEADS + 1, dtype=jnp.float32)
    return jnp.exp2(-8.0 * h / N_HEADS)


def conv_module(h, prev, w_pw1, b_pw1, w_dw, b_dw, ln_g, ln_b, w_pw2, b_pw2):
    a = h @ w_pw1 + b_pw1
    g = a[..., :D_MODEL] * jax.nn.sigmoid(a[..., D_MODEL:])
    if prev is None:
        prev = jnp.zeros((g.shape[0], CONV_W - 1, D_MODEL), g.dtype)
    gp = jnp.concatenate([prev.astype(g.dtype), g], axis=1)
    y = lax.conv_general_dilated(gp, w_dw.astype(g.dtype)[:, None, :], window_strides=(1,), padding='VALID',
                                 dimension_numbers=('NWC', 'WIO', 'NWC'), feature_group_count=D_MODEL)
    y = y + b_dw
    y = jax.nn.silu(layer_norm(y, ln_g, ln_b))
    out = y @ w_pw2 + b_pw2
    return out, gp[:, -(CONV_W - 1):]


def split_qkv(h, w_qkv):
    B, T, _ = h.shape
    qkv = h @ w_qkv
    q = qkv[..., :N_HEADS * HEAD_DIM].reshape(B, T, N_HEADS, HEAD_DIM)
    k = qkv[..., N_HEADS * HEAD_DIM:(N_HEADS + N_KV) * HEAD_DIM].reshape(B, T, N_KV, HEAD_DIM)
    v = qkv[..., (N_HEADS + N_KV) * HEAD_DIM:].reshape(B, T, N_KV, HEAD_DIM)
    return q, k, v


def band_attention(q, k, v, q_pos, k_pos, valid, sinks):
    B, N, Q = q.shape[:3]
    qg = q.reshape(B, N, Q, N_KV, GROUP, HEAD_DIM)
    s = jnp.einsum('bnqkgd,bnskd->bnkgqs', qg, k).astype(jnp.float32) * (HEAD_DIM ** -0.5)
    dist = jnp.abs(q_pos[:, :, None] - k_pos[:, None, :]).astype(jnp.float32)
    slopes = alibi_slopes().reshape(N_KV, GROUP)
    s = s - slopes[None, None, :, :, None, None] * dist[None, :, None, None]
    s = jnp.where(valid[None, :, None, None, None, :], s, NEG)
    sink = sinks.astype(jnp.float32).reshape(N_KV, GROUP)[None, None, :, :, None, None]
    m = jnp.maximum(jnp.max(s, axis=-1, keepdims=True), sink)
    p = jnp.exp(s - m)
    p = p / (jnp.sum(p, axis=-1, keepdims=True) + jnp.exp(sink - m))
    o = jnp.einsum('bnkgqs,bnskd->bnqkgd', p.astype(v.dtype), v)
    return o.reshape(B, N, Q, N_HEADS * HEAD_DIM)


def attn_prompt(h, w_qkv, w_o, sinks):
    B, T, _ = h.shape
    nc = T // CHUNK
    q, k, v = split_qkv(h, w_qkv)
    pad = ((0, 0), (WINDOW, 0), (0, 0), (0, 0))
    kp = jnp.pad(k, pad).reshape(B, nc + WIN_CHUNKS, CHUNK, N_KV, HEAD_DIM)
    vp = jnp.pad(v, pad).reshape(B, nc + WIN_CHUNKS, CHUNK, N_KV, HEAD_DIM)
    kb = jnp.concatenate([kp[:, j:j + nc] for j in range(WIN_CHUNKS + 1)], axis=2)
    vb = jnp.concatenate([vp[:, j:j + nc] for j in range(WIN_CHUNKS + 1)], axis=2)
    qb = q.reshape(B, nc, CHUNK, N_HEADS, HEAD_DIM)
    q_pos = jnp.arange(T, dtype=jnp.int32).reshape(nc, CHUNK)
    k_pos = (jnp.arange(nc, dtype=jnp.int32)[:, None] - WIN_CHUNKS) * CHUNK + jnp.arange(BAND, dtype=jnp.int32)[None, :]
    o = band_attention(qb, kb, vb, q_pos, k_pos, k_pos >= 0, sinks).reshape(B, T, N_HEADS * HEAD_DIM)
    return o @ w_o, k[:, -WINDOW:], v[:, -WINDOW:]


def attn_sample(h, ck, cv, w_qkv, w_o, sinks):
    B, T, _ = h.shape
    q, k, v = split_qkv(h, w_qkv)
    kk = jnp.concatenate([ck.astype(k.dtype), k], axis=1)
    vv = jnp.concatenate([cv.astype(v.dtype), v], axis=1)
    q_pos = (PAST_LEN + jnp.arange(T, dtype=jnp.int32))[None]
    k_pos = jnp.concatenate([PAST_LEN - WINDOW + jnp.arange(WINDOW, dtype=jnp.int32),
                             PAST_LEN + jnp.arange(T, dtype=jnp.int32)])[None]
    o = band_attention(q[:, None], kk[:, None], vv[:, None], q_pos, k_pos, k_pos >= 0, sinks)
    o = o.reshape(B, T, N_HEADS * HEAD_DIM)
    return o @ w_o, kk[:, -WINDOW:], vv[:, -WINDOW:]


def swiglu(h, w_in, w_out):
    a = h @ w_in
    return (jax.nn.silu(a[..., :D_FF]) * a[..., D_FF:]) @ w_out


def trunk(x, c, prompt, state_conv, cache_k, cache_v, norm_mix_g, norm_ffn_g, w_ada, b_ada,
          w_pw1, b_pw1, w_dw, b_dw, ln_g, ln_b, w_pw2, b_pw2, w_qkv, w_o, sinks,
          w_ffn_in, w_ffn_out, final_norm_g):
    conv_new, k_new, v_new = [], [], []
    for i in range(DEPTH):
        mod = jax.nn.silu(c) @ w_ada[i] + b_ada[i]
        sh1, sc1, g1, sh2, sc2, g2 = jnp.split(mod, 6, axis=-1)
        h = modulate(rms_norm(x, norm_mix_g[i]), sh1, sc1)
        j = i // 2
        if i % 2 == 0:
            prev = None if prompt else state_conv[j]
            y, st = conv_module(h, prev, w_pw1[j], b_pw1[j], w_dw[j], b_dw[j], ln_g[j], ln_b[j], w_pw2[j], b_pw2[j])
            conv_new.append(st)
        else:
            if prompt:
                y, ks, vs = attn_prompt(h, w_qkv[j], w_o[j], sinks[j])
            else:
                y, ks, vs = attn_sample(h, cache_k[j], cache_v[j], w_qkv[j], w_o[j], sinks[j])
            k_new.append(ks)
            v_new.append(vs)
        x = x + g1[:, None, :] * y
        h = modulate(rms_norm(x, norm_ffn_g[i]), sh2, sc2)
        x = x + g2[:, None, :] * swiglu(h, w_ffn_in[i], w_ffn_out[i])
    return rms_norm(x, final_norm_g), jnp.stack(conv_new), jnp.stack(k_new), jnp.stack(v_new)


def setup_inputs(seed: int = 0) -> dict:
    key = jax.random.key(seed)
    ks = jax.random.split(key, 28)
    f32 = jnp.float32

    def nrm(k, shape, scale=1.0):
        return jax.random.normal(k, shape, f32) * scale

    return {
        "x_prompt": nrm(ks[0], (BATCH, SEQ, D_MODEL)),
        "x_sample": nrm(ks[1], (DEC_BATCH, DEC_SEQ, D_MODEL)),
        "c_prompt": nrm(ks[2], (BATCH, D_MODEL)),
        "c_sample": nrm(ks[3], (DEC_BATCH, D_MODEL)),
        "state_conv": nrm(ks[4], (N_A, DEC_BATCH, CONV_W - 1, D_MODEL), 0.5),
        "cache_k": nrm(ks[5], (N_B, DEC_BATCH, WINDOW, N_KV, HEAD_DIM)),
        "cache_v": nrm(ks[6], (N_B, DEC_BATCH, WINDOW, N_KV, HEAD_DIM)),
        "norm_mix_g": 1.0 + nrm(ks[7], (DEPTH, D_MODEL), 0.02),
        "norm_ffn_g": 1.0 + nrm(ks[8], (DEPTH, D_MODEL), 0.02),
        "w_ada": nrm(ks[9], (DEPTH, D_MODEL, 6 * D_MODEL), 0.5 * D_MODEL ** -0.5),
        "b_ada": nrm(ks[10], (DEPTH, 6 * D_MODEL), 0.02),
        "w_pw1": nrm(ks[11], (N_A, D_MODEL, 2 * D_MODEL), D_MODEL ** -0.5),
        "b_pw1": nrm(ks[12], (N_A, 2 * D_MODEL), 0.02),
        "w_dw": nrm(ks[13], (N_A, CONV_W, D_MODEL), CONV_W ** -0.5),
        "b_dw": nrm(ks[14], (N_A, D_MODEL), 0.02),
        "ln_g": 1.0 + nrm(ks[15], (N_A, D_MODEL), 0.02),
        "ln_b": nrm(ks[16], (N_A, D_MODEL), 0.02),
        "w_pw2": nrm(ks[17], (N_A, D_MODEL, D_MODEL), D_MODEL ** -0.5),
        "b_pw2": nrm(ks[18], (N_A, D_MODEL), 0.02),
        "w_qkv": nrm(ks[19], (N_B, D_MODEL, QKV_COLS), D_MODEL ** -0.5),
        "w_o": nrm(ks[20], (N_B, N_HEADS * HEAD_DIM, D_MODEL), (N_HEADS * HEAD_DIM) ** -0.5),
        "sinks": nrm(ks[21], (N_B, N_HEADS), 1.0),
        "w_ffn_in": nrm(ks[22], (DEPTH, D_MODEL, 2 * D_FF), D_MODEL ** -0.5),
        "w_ffn_out": nrm(ks[23], (DEPTH, D_FF, D_MODEL), D_FF ** -0.5),
        "final_norm_g": 1.0 + nrm(ks[24], (D_MODEL,), 0.02),
    }


def reference(x_prompt, x_sample, c_prompt, c_sample, state_conv, cache_k, cache_v,
              norm_mix_g, norm_ffn_g, w_ada, b_ada, w_pw1, b_pw1, w_dw, b_dw, ln_g, ln_b,
              w_pw2, b_pw2, w_qkv, w_o, sinks, w_ffn_in, w_ffn_out, final_norm_g):
    weights = (norm_mix_g, norm_ffn_g, w_ada, b_ada, w_pw1, b_pw1, w_dw, b_dw, ln_g, ln_b,
               w_pw2, b_pw2, w_qkv, w_o, sinks, w_ffn_in, w_ffn_out, final_norm_g)
    y_prompt, conv_prompt, k_prompt, v_prompt = trunk(x_prompt, c_prompt, True, None, None, None, *weights)
    y_sample, conv_sample, k_sample, v_sample = trunk(x_sample, c_sample, False, state_conv, cache_k, cache_v, *weights)
    return (y_prompt, y_sample, conv_prompt, conv_sample, k_prompt, v_prompt, k_sample, v_sample)
```

```python
import functools

import jax
import jax.numpy as jnp
from jax import lax
from jax.experimental import pallas as pl
from jax.experimental.pallas import tpu as pltpu

F32 = jnp.float32
BF16 = jnp.bfloat16

D_MODEL = 2048
DEPTH = 2
CHUNK = 64
CONV_W = 31
N_HEADS = 32
N_KV = 4
HEAD_DIM = 64
GROUP = N_HEADS // N_KV
WINDOW = 128
D_FF = 5632
EPS = 1e-6
NEG = -1e30

VMEM_LIMIT_BYTES = 60 * 1024 * 1024
LANES = 128
HALO = 32
PAIR = 2 * HEAD_DIM
KPAD = 256


def _params(sem):
    return pltpu.CompilerParams(dimension_semantics=sem, vmem_limit_bytes=VMEM_LIMIT_BYTES)


def _silu(v):
    return v * jax.nn.sigmoid(v)


def _norm_modulate(x, gain, scale, shift):
    y = x * lax.rsqrt(jnp.mean(x * x, axis=-1, keepdims=True) + EPS)
    return (y * gain) * (1.0 + scale) + shift


def _fill_h(h_scr, x_ref, gain_ref, scale_ref, shift_ref, nb):
    rows = x_ref.shape[0] // nb
    for s in range(nb):
        sl = slice(s * rows, (s + 1) * rows)
        h = _norm_modulate(x_ref[sl, :], gain_ref[...], scale_ref[s], shift_ref[s])
        h_scr[sl, :] = h.astype(BF16)


def _adaln_kernel(c_ref, w_ref, b_ref, o_ref):
    c = c_ref[...]
    a = _silu(c).astype(BF16)
    o_ref[0] = jnp.dot(a, w_ref[0].astype(BF16), preferred_element_type=F32) + b_ref[0]


def _adaln(c_all, w_ada, b_ada, tn=1024):
    bp = c_all.shape[0]
    n = w_ada.shape[-1]
    return pl.pallas_call(
        _adaln_kernel,
        grid=(DEPTH, n // tn),
        in_specs=[
            pl.BlockSpec((bp, D_MODEL), lambda l, j: (0, 0)),
            pl.BlockSpec((1, D_MODEL, tn), lambda l, j: (l, 0, j)),
            pl.BlockSpec((1, 1, tn), lambda l, j: (l, 0, j)),
        ],
        out_specs=pl.BlockSpec((1, bp, tn), lambda l, j: (l, 0, j)),
        out_shape=jax.ShapeDtypeStruct((DEPTH, bp, n), F32),
        compiler_params=_params(("arbitrary", "arbitrary")),
        name="adaln",
    )(c_all, w_ada, b_ada.reshape(DEPTH, 1, n))


def _row_tiling(seq_len, tm):
    if tm >= seq_len:
        assert tm % seq_len == 0
        return tm // seq_len, 1
    assert seq_len % tm == 0
    return 1, seq_len // tm


def _pw1_glu_kernel(x_ref, gain_ref, scale_ref, shift_ref, wa_ref, wb_ref, ba_ref, bb_ref,
                    g_ref, h_scr, *, nb):
    @pl.when(pl.program_id(1) == 0)
    def _():
        _fill_h(h_scr, x_ref, gain_ref, scale_ref, shift_ref, nb)

    h = h_scr[...]
    a = jnp.dot(h, wa_ref[...], preferred_element_type=F32) + ba_ref[...]
    b = jnp.dot(h, wb_ref[...], preferred_element_type=F32) + bb_ref[...]
    g_ref[...] = a * jax.nn.sigmoid(b)


def _pw1_glu(x2, seq_len, gain, scale, shift, w1, b1, tm=1024, tn=512):
    rows = x2.shape[0]
    nb, tps = _row_tiling(seq_len, tm)
    nj = D_MODEL // tn
    mod_spec = pl.BlockSpec((nb, 1, D_MODEL), lambda i, j: (i // tps, 0, 0))
    return pl.pallas_call(
        functools.partial(_pw1_glu_kernel, nb=nb),
        grid=(rows // tm, nj),
        in_specs=[
            pl.BlockSpec((tm, D_MODEL), lambda i, j: (i, 0)),
            pl.BlockSpec((1, D_MODEL), lambda i, j: (0, 0)),
            mod_spec, mod_spec,
            pl.BlockSpec((D_MODEL, tn), lambda i, j: (0, j)),
            pl.BlockSpec((D_MODEL, tn), lambda i, j: (0, nj + j)),
            pl.BlockSpec((1, tn), lambda i, j: (0, j)),
            pl.BlockSpec((1, tn), lambda i, j: (0, nj + j)),
        ],
        out_specs=pl.BlockSpec((tm, tn), lambda i, j: (i, j)),
        out_shape=jax.ShapeDtypeStruct((rows, D_MODEL), F32),
        scratch_shapes=[pltpu.VMEM((tm, D_MODEL), BF16)],
        compiler_params=_params(("arbitrary", "arbitrary")),
        name="pw1_glu",
    )(x2, gain, scale, shift, w1, w1, b1, b1)


CONV_ROWS = 64
LN_ROWS = 32
NCHUNK = D_MODEL // LANES


def _conv_pw2_kernel(g_ref, halo_ref, prev_ref, x_ref, gate_ref, wdw_ref, bdw_ref, lng_ref, lnb_ref,
                     w2_ref, b2_ref, o_ref, win_scr, y_scr, ybf_scr, *, nb, tps):
    i = pl.program_id(0)
    tm = g_ref.shape[0]
    tt = tm // nb

    @pl.when(pl.program_id(1) == 0)
    def _():
        def put_history(src_of):
            for s in range(nb):
                for c in range(NCHUNK):
                    win_scr[s * NCHUNK + c, 0:HALO, :] = src_of(s, c)

        if tps == 1:
            put_history(lambda s, c: prev_ref[s, :, c * LANES:(c + 1) * LANES])
        else:
            first = (i % tps) == 0

            @pl.when(first)
            def _():
                put_history(lambda s, c: prev_ref[s, :, c * LANES:(c + 1) * LANES])

            @pl.when(jnp.logical_not(first))
            def _():
                put_history(lambda s, c: halo_ref[:, c * LANES:(c + 1) * LANES])
        for s in range(nb):
            for c in range(NCHUNK):
                win_scr[s * NCHUNK + c, HALO:HALO + tt, :] = g_ref[s * tt:(s + 1) * tt, c * LANES:(c + 1) * LANES]

        lead = HALO - (CONV_W - 1)

        def conv_chunk(idx, carry):
            s = idx // NCHUNK
            c = idx % NCHUNK
            for rb in range(tt // CONV_ROWS):
                acc = jnp.zeros((CONV_ROWS, LANES), F32) + bdw_ref[c]
                for k in range(CONV_W):
                    r0 = rb * CONV_ROWS + k + lead
                    acc = acc + win_scr[idx, r0:r0 + CONV_ROWS, :] * wdw_ref[c, k:k + 1, :]
                dst = pl.multiple_of(s * tt + rb * CONV_ROWS, CONV_ROWS)
                y_scr[c, pl.ds(dst, CONV_ROWS), :] = acc
            return carry

        lax.fori_loop(0, nb * NCHUNK, conv_chunk, 0)

        def ln_block(r, carry):
            rs = pl.ds(pl.multiple_of(r * LN_ROWS, LN_ROWS), LN_ROWS)
            ys = [y_scr[c, rs, :] for c in range(NCHUNK)]
            tot = ys[0]
            for c in range(1, NCHUNK):
                tot = tot + ys[c]
            mu = jnp.sum(tot, axis=-1, keepdims=True) * (1.0 / D_MODEL)
            ycs = [y - mu for y in ys]
            sq = ycs[0] * ycs[0]
            for c in range(1, NCHUNK):
                sq = sq + ycs[c] * ycs[c]
            rstd = lax.rsqrt(jnp.sum(sq, axis=-1, keepdims=True) * (1.0 / D_MODEL) + EPS)
            for c in range(NCHUNK):
                cs = slice(c * LANES, (c + 1) * LANES)
                z = ycs[c] * rstd * lng_ref[:, cs] + lnb_ref[:, cs]
                ybf_scr[rs, cs] = _silu(z).astype(BF16)
            return carry

        lax.fori_loop(0, tm // LN_ROWS, ln_block, 0)

    out = jnp.dot(ybf_scr[...], w2_ref[...], preferred_element_type=F32) + b2_ref[...]
    for s in range(nb):
        sl = slice(s * tt, (s + 1) * tt)
        o_ref[sl, :] = x_ref[sl, :] + gate_ref[s] * out[sl, :]


def _conv_pw2(g2, prev, x2, seq_len, gate, w_dw, b_dw, ln_g, ln_b, w2, b2, tm, tn=512):
    rows = g2.shape[0]
    nb, tps = _row_tiling(seq_len, tm)
    tt = tm // nb
    hb = tm // HALO
    return pl.pallas_call(
        functools.partial(_conv_pw2_kernel, nb=nb, tps=tps),
        grid=(rows // tm, D_MODEL // tn),
        in_specs=[
            pl.BlockSpec((tm, D_MODEL), lambda i, j: (i, 0)),
            pl.BlockSpec((HALO, D_MODEL), lambda i, j: (jnp.maximum(i * hb - 1, 0), 0)),
            pl.BlockSpec((nb, HALO, D_MODEL), lambda i, j: (i // tps, 0, 0)),
            pl.BlockSpec((tm, tn), lambda i, j: (i, j)),
            pl.BlockSpec((nb, 1, tn), lambda i, j: (i // tps, 0, j)),
            pl.BlockSpec((NCHUNK, CONV_W, LANES), lambda i, j: (0, 0, 0)),
            pl.BlockSpec((NCHUNK, 1, LANES), lambda i, j: (0, 0, 0)),
            pl.BlockSpec((1, D_MODEL), lambda i, j: (0, 0)),
            pl.BlockSpec((1, D_MODEL), lambda i, j: (0, 0)),
            pl.BlockSpec((D_MODEL, tn), lambda i, j: (0, j)),
            pl.BlockSpec((1, tn), lambda i, j: (0, j)),
        ],
        out_specs=pl.BlockSpec((tm, tn), lambda i, j: (i, j)),
        out_shape=jax.ShapeDtypeStruct((rows, D_MODEL), F32),
        scratch_shapes=[
            pltpu.VMEM((nb * NCHUNK, HALO + tt, LANES), F32),
            pltpu.VMEM((NCHUNK, tm, LANES), F32),
            pltpu.VMEM((tm, D_MODEL), BF16),
        ],
        compiler_params=_params(("arbitrary", "arbitrary")),
        name="conv_pw2",
    )(g2, g2, prev, x2, gate,
      w_dw.reshape(CONV_W, NCHUNK, LANES).transpose(1, 0, 2), b_dw.reshape(NCHUNK, 1, LANES),
      ln_g, ln_b, w2, b2)


FFN_SUB = 512


def _ffn_kernel(x_ref, gain_ref, scale_ref, shift_ref, gate_ref, wa_ref, wb_ref, wo_ref, fin_ref,
                o_ref, h_scr, *, nb, final_norm):
    f = pl.program_id(1)
    tm = x_ref.shape[0]

    @pl.when(f == 0)
    def _():
        _fill_h(h_scr, x_ref, gain_ref, scale_ref, shift_ref, nb)
        o_ref[...] = jnp.zeros_like(o_ref)

    for m in range(tm // FFN_SUB):
        ms = slice(m * FFN_SUB, (m + 1) * FFN_SUB)
        h = h_scr[ms, :]
        a = jnp.dot(h, wa_ref[...], preferred_element_type=F32)
        b = jnp.dot(h, wb_ref[...], preferred_element_type=F32)
        act = (_silu(a) * b).astype(BF16)
        o_ref[ms, :] += jnp.dot(act, wo_ref[...], preferred_element_type=F32)

    @pl.when(f == pl.num_programs(1) - 1)
    def _():
        rows = tm // nb
        for s in range(nb):
            sl = slice(s * rows, (s + 1) * rows)
            y = x_ref[sl, :] + gate_ref[s] * o_ref[sl, :]
            if final_norm:
                y = y * lax.rsqrt(jnp.mean(y * y, axis=-1, keepdims=True) + EPS) * fin_ref[...]
            o_ref[sl, :] = y


def _ffn(x2, seq_len, gain, scale, shift, gate, w_in, w_out, fin_gain, final_norm, tm=1024, tf=512):
    rows = x2.shape[0]
    nb, tps = _row_tiling(seq_len, tm)
    nf = D_FF // tf
    mod_spec = pl.BlockSpec((nb, 1, D_MODEL), lambda i, f: (i // tps, 0, 0))
    vec_spec = pl.BlockSpec((1, D_MODEL), lambda i, f: (0, 0))
    return pl.pallas_call(
        functools.partial(_ffn_kernel, nb=nb, final_norm=final_norm),
        grid=(rows // tm, nf),
        in_specs=[
            pl.BlockSpec((tm, D_MODEL), lambda i, f: (i, 0), pipeline_mode=pl.Buffered(1)),
            vec_spec, mod_spec, mod_spec, mod_spec,
            pl.BlockSpec((D_MODEL, tf), lambda i, f: (0, f)),
            pl.BlockSpec((D_MODEL, tf), lambda i, f: (0, nf + f)),
            pl.BlockSpec((tf, D_MODEL), lambda i, f: (f, 0)),
            vec_spec,
        ],
        out_specs=pl.BlockSpec((tm, D_MODEL), lambda i, f: (i, 0)),
        out_shape=jax.ShapeDtypeStruct((rows, D_MODEL), F32),
        scratch_shapes=[pltpu.VMEM((tm, D_MODEL), BF16)],
        compiler_params=_params(("arbitrary", "arbitrary")),
        name="ffn",
    )(x2, gain, scale, shift, gate, w_in, w_in, w_out, fin_gain)


def _qkv_kernel(x_ref, gain_ref, scale_ref, shift_ref, wq_ref, wkv_ref, q_ref, kv_ref, h_scr, *, nb):
    @pl.when(pl.program_id(1) == 0)
    def _():
        _fill_h(h_scr, x_ref, gain_ref, scale_ref, shift_ref, nb)
        kv_ref[...] = jnp.dot(h_scr[...], wkv_ref[...], preferred_element_type=F32)

    q = jnp.dot(h_scr[...], wq_ref[...], preferred_element_type=F32)
    q_ref[...] = (q * (HEAD_DIM ** -0.5)).astype(BF16)


def _qkv(x2, seq_len, gain, scale, shift, w_qkv, tm=1024, tn=512):
    rows = x2.shape[0]
    nb, tps = _row_tiling(seq_len, tm)
    nq = (N_HEADS * HEAD_DIM) // tn
    kv_cols = 2 * N_KV * HEAD_DIM
    assert kv_cols == tn
    mod_spec = pl.BlockSpec((nb, 1, D_MODEL), lambda i, j: (i // tps, 0, 0))
    return pl.pallas_call(
        functools.partial(_qkv_kernel, nb=nb),
        grid=(rows // tm, nq),
        in_specs=[
            pl.BlockSpec((tm, D_MODEL), lambda i, j: (i, 0)),
            pl.BlockSpec((1, D_MODEL), lambda i, j: (0, 0)),
            mod_spec, mod_spec,
            pl.BlockSpec((D_MODEL, tn), lambda i, j: (0, j)),
            pl.BlockSpec((D_MODEL, kv_cols), lambda i, j: (0, nq)),
        ],
        out_specs=[
            pl.BlockSpec((tm, tn), lambda i, j: (i, j)),
            pl.BlockSpec((tm, kv_cols), lambda i, j: (i, 0)),
        ],
        out_shape=[
            jax.ShapeDtypeStruct((rows, N_HEADS * HEAD_DIM), BF16),
            jax.ShapeDtypeStruct((rows, kv_cols), F32),
        ],
        scratch_shapes=[pltpu.VMEM((tm, D_MODEL), BF16)],
        compiler_params=_params(("arbitrary", "arbitrary")),
        name="qkv",
    )(x2, gain, scale, shift, w_qkv, w_qkv)


PAIRS_PER_KV = GROUP // 2
QROWS = PAIRS_PER_KV * CHUNK
BAND = WINDOW + CHUNK


def _attn_kernel(slope_ref, sink_ref, q_ref, k0_ref, k1_ref, k2_ref, v0_ref, v1_ref, v2_ref, o_ref,
                 bias_scr, sink_scr, *, chunks_per_seq, masked):
    step = pl.program_id(0)
    row = lax.broadcasted_iota(jnp.int32, (QROWS, 2 * KPAD), 0)
    col = lax.broadcasted_iota(jnp.int32, (QROWS, 2 * KPAD), 1)
    key = jnp.bitwise_and(col, KPAD - 1)
    lane = lax.broadcasted_iota(jnp.int32, (QROWS, PAIR), 1)
    first_half = lane < HEAD_DIM

    @pl.when(step == 0)
    def _():
        dist = jnp.abs(WINDOW + jnp.bitwise_and(row, CHUNK - 1) - key).astype(F32)
        prow = lax.broadcasted_iota(jnp.int32, (QROWS, PAIR), 0)
        for kv in range(N_KV):
            slope = jnp.zeros((QROWS, 2 * KPAD), F32)
            sink = jnp.zeros((QROWS, PAIR), F32)
            for pr in range(PAIRS_PER_KV):
                for ab in range(2):
                    head = kv * GROUP + pr * 2 + ab
                    in_pair = jnp.logical_and(row >= pr * CHUNK, row < (pr + 1) * CHUNK)
                    in_half = (col >= KPAD) if ab else (col < KPAD)
                    slope = jnp.where(jnp.logical_and(in_pair, in_half), slope_ref[head], slope)
                    in_pair2 = jnp.logical_and(prow >= pr * CHUNK, prow < (pr + 1) * CHUNK)
                    in_half2 = (lane >= HEAD_DIM) if ab else first_half
                    sink = jnp.where(jnp.logical_and(in_pair2, in_half2), sink_ref[head], sink)
            bias_scr[kv] = jnp.where(key < BAND, -(slope * dist), NEG)
            sink_scr[kv] = sink

    kcat = jnp.concatenate([k0_ref[...], k1_ref[...], k2_ref[...]], axis=0)
    vcat = jnp.concatenate([v0_ref[...], v1_ref[...], v2_ref[...]], axis=0)
    lane_k = lax.broadcasted_iota(jnp.int32, (BAND, PAIR), 1)
    pad = jnp.zeros((KPAD - BAND, PAIR), F32)

    def block_diag(cat, kv):
        blk = cat[:, (kv // 2) * PAIR:(kv // 2 + 1) * PAIR]
        if kv % 2 == 0:
            left = jnp.where(lane_k < HEAD_DIM, blk, 0.0)
            right = pltpu.roll(left, HEAD_DIM, axis=1)
        else:
            right = jnp.where(lane_k >= HEAD_DIM, blk, 0.0)
            left = pltpu.roll(right, HEAD_DIM, axis=1)
        return jnp.concatenate([left, pad, right, pad], axis=0).astype(BF16)

    if masked:
        n = step % chunks_per_seq
        valid = key >= WINDOW - n * CHUNK

    for kv in range(N_KV):
        kbd = block_diag(kcat, kv)
        vbd = block_diag(vcat, kv)
        qs = jnp.concatenate(
            [q_ref[:, kv * GROUP * HEAD_DIM + pr * PAIR: kv * GROUP * HEAD_DIM + (pr + 1) * PAIR]
             for pr in range(PAIRS_PER_KV)], axis=0)
        s = lax.dot_general(qs, kbd, (((1,), (1,)), ((), ())), preferred_element_type=F32)
        s = s + bias_scr[kv]
        if masked:
            s = jnp.where(valid, s, NEG)
        sink = sink_scr[kv]
        m_a = jnp.max(s[:, :KPAD], axis=-1, keepdims=True)
        m_b = jnp.max(s[:, KPAD:], axis=-1, keepdims=True)
        m2 = jnp.maximum(jnp.where(first_half, m_a, m_b), sink)
        m_a = m2[:, 0:1]
        m_b = m2[:, HEAD_DIM:HEAD_DIM + 1]
        p_a = jnp.exp(s[:, :KPAD] - m_a)
        p_b = jnp.exp(s[:, KPAD:] - m_b)
        l_a = jnp.sum(p_a, axis=-1, keepdims=True)
        l_b = jnp.sum(p_b, axis=-1, keepdims=True)
        denom = jnp.where(first_half, l_a, l_b) + jnp.exp(sink - m2)
        p = jnp.concatenate([p_a, p_b], axis=1).astype(BF16)
        o = jnp.dot(p, vbd, preferred_element_type=F32) / denom
        for pr in range(PAIRS_PER_KV):
            c0 = kv * GROUP * HEAD_DIM + pr * PAIR
            o_ref[:, c0:c0 + PAIR] = o[pr * CHUNK:(pr + 1) * CHUNK, :].astype(BF16)


def _attention(q, k_srcs, v_srcs, kv_maps, slopes, sinks, chunks_per_seq, masked):
    rows = q.shape[0]
    kvw = N_KV * HEAD_DIM
    smem = pl.BlockSpec(memory_space=pltpu.SMEM)
    kv_specs = [pl.BlockSpec((CHUNK, kvw), m) for m in kv_maps]
    return pl.pallas_call(
        functools.partial(_attn_kernel, chunks_per_seq=chunks_per_seq, masked=masked),
        grid=(rows // CHUNK,),
        in_specs=[smem, smem, pl.BlockSpec((CHUNK, N_HEADS * HEAD_DIM), lambda i: (i, 0))] + kv_specs,
        out_specs=pl.BlockSpec((CHUNK, N_HEADS * HEAD_DIM), lambda i: (i, 0)),
        out_shape=jax.ShapeDtypeStruct((rows, N_HEADS * HEAD_DIM), BF16),
        scratch_shapes=[
            pltpu.VMEM((N_KV, QROWS, 2 * KPAD), F32),
            pltpu.VMEM((N_KV, QROWS, PAIR), F32),
        ],
        compiler_params=_params(("arbitrary",)),
        name="attention",
    )(slopes, sinks, q, *k_srcs, *v_srcs)


def _wo_kernel(a_ref, w_ref, x_ref, gate_ref, o_ref, *, nb):
    out = jnp.dot(a_ref[...], w_ref[...], preferred_element_type=F32)
    rows = a_ref.shape[0] // nb
    for s in range(nb):
        sl = slice(s * rows, (s + 1) * rows)
        o_ref[sl, :] = x_ref[sl, :] + gate_ref[s] * out[sl, :]


def _wo(attn, x2, seq_len, gate, w_o, tm=1024, tn=512):
    rows = x2.shape[0]
    nb, tps = _row_tiling(seq_len, tm)
    return pl.pallas_call(
        functools.partial(_wo_kernel, nb=nb),
        grid=(rows // tm, D_MODEL // tn),
        in_specs=[
            pl.BlockSpec((tm, N_HEADS * HEAD_DIM), lambda i, j: (i, 0)),
            pl.BlockSpec((N_HEADS * HEAD_DIM, tn), lambda i, j: (0, j)),
            pl.BlockSpec((tm, tn), lambda i, j: (i, j)),
            pl.BlockSpec((nb, 1, tn), lambda i, j: (i // tps, 0, j)),
        ],
        out_specs=pl.BlockSpec((tm, tn), lambda i, j: (i, j)),
        out_shape=jax.ShapeDtypeStruct((rows, D_MODEL), F32),
        compiler_params=_params(("arbitrary", "arbitrary")),
        name="wo",
    )(attn, w_o, x2, gate)


def _trunk(x, mods, state_conv, cache_k, cache_v, wts):
    batch, seq_len, _ = x.shape
    rows = batch * seq_len
    x2 = x.reshape(rows, D_MODEL)
    prompt = state_conv is None
    row = lambda v: v.reshape(1, -1)

    sh1, sc1, g1, sh2, sc2, g2 = mods[0]
    g = _pw1_glu(x2, seq_len, row(wts["norm_mix_g"][0]), sc1, sh1, wts["w_pw1"], row(wts["b_pw1"][0]))
    if prompt:
        prev = jnp.zeros((batch, HALO, D_MODEL), F32)
    else:
        prev = jnp.pad(state_conv[0], ((0, 0), (HALO - (CONV_W - 1), 0), (0, 0)))
    assert seq_len >= CONV_W - 1
    conv_new = g.reshape(batch, seq_len, D_MODEL)[:, -(CONV_W - 1):][None]
    x2 = _conv_pw2(g, prev, x2, seq_len, g1, wts["w_dw"][0], row(wts["b_dw"][0]), row(wts["ln_g"][0]),
                   row(wts["ln_b"][0]), wts["w_pw2"], row(wts["b_pw2"][0]), tm=512)
    x2 = _ffn(x2, seq_len, row(wts["norm_ffn_g"][0]), sc2, sh2, g2, wts["w_ffn_in"][0], wts["w_ffn_out"][0],
              row(wts["final_norm_g"]), final_norm=False)

    sh1, sc1, g1, sh2, sc2, g2 = mods[1]
    q, kv = _qkv(x2, seq_len, row(wts["norm_mix_g"][1]), sc1, sh1, wts["w_qkv"])
    kvw = N_KV * HEAD_DIM
    if prompt:
        cps = seq_len // CHUNK
        lo = lambda d: (lambda i: (jnp.maximum(i % cps - d, 0) + (i // cps) * cps, 0))
        hi = lambda d: (lambda i: (jnp.maximum(i % cps - d, 0) + (i // cps) * cps, 1))
        attn = _attention(q, [kv, kv, kv], [kv, kv, kv], [lo(2), lo(1), lo(0), hi(2), hi(1), hi(0)],
                          wts["slopes"], wts["sinks"], cps, masked=True)
        kv3 = kv.reshape(batch, seq_len, 2 * kvw)
        k_new = kv3[:, -WINDOW:, :kvw]
        v_new = kv3[:, -WINDOW:, kvw:]
    else:
        ck = cache_k[0].reshape(batch * WINDOW, kvw)
        cv = cache_v[0].reshape(batch * WINDOW, kvw)
        maps = [lambda i: (2 * i, 0), lambda i: (2 * i + 1, 0), lambda i: (i, 0),
                lambda i: (2 * i, 0), lambda i: (2 * i + 1, 0), lambda i: (i, 1)]
        attn = _attention(q, [ck, ck, kv], [cv, cv, kv], maps, wts["slopes"], wts["sinks"], 1, masked=False)
        kv3 = kv.reshape(batch, seq_len, 2 * kvw)
        k_new = jnp.concatenate([ck.reshape(batch, WINDOW, kvw), kv3[:, :, :kvw]], axis=1)[:, -WINDOW:]
        v_new = jnp.concatenate([cv.reshape(batch, WINDOW, kvw), kv3[:, :, kvw:]], axis=1)[:, -WINDOW:]
    x2 = _wo(attn, x2, seq_len, g1, wts["w_o"])
    y2 = _ffn(x2, seq_len, row(wts["norm_ffn_g"][1]), sc2, sh2, g2, wts["w_ffn_in"][1], wts["w_ffn_out"][1],
              row(wts["final_norm_g"]), final_norm=True)

    shape_kv = (1, batch, WINDOW, N_KV, HEAD_DIM)
    return (y2.reshape(batch, seq_len, D_MODEL), conv_new,
            k_new.reshape(shape_kv), v_new.reshape(shape_kv))


def kernel(x_prompt, x_sample, c_prompt, c_sample, state_conv, cache_k, cache_v, norm_mix_g, norm_ffn_g,
           w_ada, b_ada, w_pw1, b_pw1, w_dw, b_dw, ln_g, ln_b, w_pw2, b_pw2, w_qkv, w_o, sinks,
           w_ffn_in, w_ffn_out, final_norm_g):
    bp, bs = c_prompt.shape[0], c_sample.shape[0]
    b_pad = -(-(bp + bs) // 8) * 8
    c_all = jnp.concatenate([c_prompt, c_sample, jnp.zeros((b_pad - bp - bs, D_MODEL), F32)], axis=0)
    mod = _adaln(c_all, w_ada, b_ada)

    def mods_for(lo, n):
        return [[mod[l, lo:lo + n, k * D_MODEL:(k + 1) * D_MODEL].reshape(n, 1, D_MODEL) for k in range(6)]
                for l in range(DEPTH)]

    heads = jnp.arange(1, N_HEADS + 1, dtype=F32)
    wts = dict(
        norm_mix_g=norm_mix_g, norm_ffn_g=norm_ffn_g, final_norm_g=final_norm_g,
        w_pw1=w_pw1[0].astype(BF16), b_pw1=b_pw1, w_dw=w_dw, b_dw=b_dw, ln_g=ln_g, ln_b=ln_b,
        w_pw2=w_pw2[0].astype(BF16), b_pw2=b_pw2, w_qkv=w_qkv[0].astype(BF16), w_o=w_o[0].astype(BF16),
        w_ffn_in=w_ffn_in.astype(BF16), w_ffn_out=w_ffn_out.astype(BF16),
        slopes=jnp.exp2(-8.0 * heads / N_HEADS), sinks=sinks[0].astype(F32),
    )
    y_p, conv_p, k_p, v_p = _trunk(x_prompt, mods_for(0, bp), None, None, None, wts)
    y_s, conv_s, k_s, v_s = _trunk(x_sample, mods_for(bp, bs), state_conv, cache_k, cache_v, wts)
    return (y_p, y_s, conv_p, conv_s, k_p, v_p, k_s, v_s)
```

```python
import functools

import jax
import jax.numpy as jnp
from jax import lax
from jax.experimental import pallas as pl
from jax.experimental.pallas import tpu as pltpu

F32 = jnp.float32
BF16 = jnp.bfloat16

D_MODEL = 2048
DEPTH = 2
CHUNK = 64
CONV_W = 31
N_HEADS = 32
N_KV = 4
HEAD_DIM = 64
GROUP = N_HEADS // N_KV
WINDOW = 128
D_FF = 5632
EPS = 1e-6
NEG = -1e30

VMEM_LIMIT_BYTES = 60 * 1024 * 1024
LANES = 128
HALO = 32
PAIR = 2 * HEAD_DIM
KPAD = 256

SHIFT1, SCALE1, GATE1, SHIFT2, SCALE2, GATE2 = range(6)


def _params(sem):
    return pltpu.CompilerParams(dimension_semantics=sem, vmem_limit_bytes=VMEM_LIMIT_BYTES)


def _silu(v):
    return v * jax.nn.sigmoid(v)


FILL_ROWS = 128


def _fill_h(h_scr, x_ref, gain_ref, scale_ref, shift_ref, nb):
    rows = x_ref.shape[0] // nb
    blk = min(rows, FILL_ROWS)
    for s in range(nb):
        mul = gain_ref[...] * (1.0 + scale_ref[s])
        add = shift_ref[s]

        def body(r, carry, s=s, mul=mul, add=add):
            rs = pl.ds(pl.multiple_of(s * rows + r * blk, blk), blk)
            x = x_ref[rs, :]
            y = x * lax.rsqrt(jnp.mean(x * x, axis=-1, keepdims=True) + EPS)
            h_scr[rs, :] = (y * mul + add).astype(BF16)
            return carry

        lax.fori_loop(0, rows // blk, body, 0)


def _adaln_kernel(c_ref, w_ref, b_ref, o_ref):
    c = c_ref[...]
    a = _silu(c).astype(BF16)
    o_ref[0] = jnp.dot(a, w_ref[0].astype(BF16), preferred_element_type=F32) + b_ref[0]


def _adaln(c_all, w_ada, b_ada, tn=1024):
    bp = c_all.shape[0]
    n = w_ada.shape[-1]
    return pl.pallas_call(
        _adaln_kernel,
        grid=(DEPTH, n // tn),
        in_specs=[
            pl.BlockSpec((bp, D_MODEL), lambda l, j: (0, 0)),
            pl.BlockSpec((1, D_MODEL, tn), lambda l, j: (l, 0, j)),
            pl.BlockSpec((1, 1, tn), lambda l, j: (l, 0, j)),
        ],
        out_specs=pl.BlockSpec((1, bp, tn), lambda l, j: (l, 0, j)),
        out_shape=jax.ShapeDtypeStruct((DEPTH, bp, n), F32),
        compiler_params=_params(("arbitrary", "arbitrary")),
        name="adaln",
    )(c_all, w_ada, b_ada.reshape(DEPTH, 1, n))


def _row_tiling(seq_len, tm):
    if tm >= seq_len:
        assert tm % seq_len == 0
        return tm // seq_len, 1
    assert seq_len % tm == 0
    return 1, seq_len // tm


def _mod_spec(which, row0, nb, tps, width=D_MODEL):
    assert row0 % nb == 0
    per = D_MODEL // width
    if per == 1:
        return pl.BlockSpec((nb, 1, width), lambda i, j: (row0 // nb + i // tps, 0, which))
    return pl.BlockSpec((nb, 1, width), lambda i, j: (row0 // nb + i // tps, 0, which * per + j))


def _pw1_glu_kernel(x_ref, gain_ref, scale_ref, shift_ref, wa_ref, wb_ref, ba_ref, bb_ref,
                    g_ref, h_scr, *, nb):
    @pl.when(pl.program_id(1) == 0)
    def _():
        _fill_h(h_scr, x_ref, gain_ref, scale_ref, shift_ref, nb)

    h = h_scr[...]
    a = jnp.dot(h, wa_ref[...], preferred_element_type=F32) + ba_ref[...]
    b = jnp.dot(h, wb_ref[...], preferred_element_type=F32) + bb_ref[...]
    g_ref[...] = a * jax.nn.sigmoid(b)


def _pw1_glu(x2, seq_len, gain, mod, row0, w1, b1, tm=1024, tn=512):
    rows = x2.shape[0]
    nb, tps = _row_tiling(seq_len, tm)
    nj = D_MODEL // tn
    return pl.pallas_call(
        functools.partial(_pw1_glu_kernel, nb=nb),
        grid=(rows // tm, nj),
        in_specs=[
            pl.BlockSpec((tm, D_MODEL), lambda i, j: (i, 0)),
            pl.BlockSpec((1, D_MODEL), lambda i, j: (0, 0)),
            _mod_spec(SCALE1, row0, nb, tps), _mod_spec(SHIFT1, row0, nb, tps),
            pl.BlockSpec((D_MODEL, tn), lambda i, j: (0, j)),
            pl.BlockSpec((D_MODEL, tn), lambda i, j: (0, nj + j)),
            pl.BlockSpec((1, tn), lambda i, j: (0, j)),
            pl.BlockSpec((1, tn), lambda i, j: (0, nj + j)),
        ],
        out_specs=pl.BlockSpec((tm, tn), lambda i, j: (i, j)),
        out_shape=jax.ShapeDtypeStruct((rows, D_MODEL), F32),
        scratch_shapes=[pltpu.VMEM((tm, D_MODEL), BF16)],
        compiler_params=_params(("arbitrary", "arbitrary")),
        name="pw1_glu",
    )(x2, gain, mod, mod, w1, w1, b1, b1)


CONV_ROWS = 64
LN_ROWS = 128
NCHUNK = D_MODEL // LANES


def _conv_pw2_kernel(g_ref, halo_ref, prev_ref, x_ref, gate_ref, wdw_ref, bdw_ref, lng_ref, lnb_ref,
                     w2_ref, b2_ref, o_ref, win_scr, y_scr, ybf_scr, *, nb, tps):
    i = pl.program_id(0)
    tm = g_ref.shape[0]
    tt = tm // nb

    @pl.when(pl.program_id(1) == 0)
    def _():
        def put_history(src_of):
            for s in range(nb):
                for c in range(NCHUNK):
                    win_scr[s * NCHUNK + c, 0:HALO, :] = src_of(s, c)

        if tps == 1:
            put_history(lambda s, c: prev_ref[s, :, c * LANES:(c + 1) * LANES])
        else:
            first = (i % tps) == 0

            @pl.when(first)
            def _():
                put_history(lambda s, c: prev_ref[s, :, c * LANES:(c + 1) * LANES])

            @pl.when(jnp.logical_not(first))
            def _():
                put_history(lambda s, c: halo_ref[:, c * LANES:(c + 1) * LANES])
        for s in range(nb):
            for c in range(NCHUNK):
                win_scr[s * NCHUNK + c, HALO:HALO + tt, :] = g_ref[s * tt:(s + 1) * tt, c * LANES:(c + 1) * LANES]

        lead = HALO - (CONV_W - 1)

        def conv_chunk(idx, carry):
            s = idx // NCHUNK
            c = idx % NCHUNK
            for rb in range(tt // CONV_ROWS):
                acc = jnp.zeros((CONV_ROWS, LANES), F32) + bdw_ref[c]
                for k in range(CONV_W):
                    r0 = rb * CONV_ROWS + k + lead
                    acc = acc + win_scr[idx, r0:r0 + CONV_ROWS, :] * wdw_ref[c, k:k + 1, :]
                dst = pl.multiple_of(s * tt + rb * CONV_ROWS, CONV_ROWS)
                y_scr[c, pl.ds(dst, CONV_ROWS), :] = acc
            return carry

        lax.fori_loop(0, nb * NCHUNK, conv_chunk, 0)

        def ln_block(r, carry):
            rs = pl.ds(pl.multiple_of(r * LN_ROWS, LN_ROWS), LN_ROWS)
            tot = y_scr[0, rs, :]
            for c in range(1, NCHUNK):
                tot = tot + y_scr[c, rs, :]
            mu = jnp.sum(tot, axis=-1, keepdims=True) * (1.0 / D_MODEL)
            sq = jnp.zeros((LN_ROWS, LANES), F32)
            for c in range(NCHUNK):
                yc = y_scr[c, rs, :] - mu
                sq = sq + yc * yc
            rstd = lax.rsqrt(jnp.sum(sq, axis=-1, keepdims=True) * (1.0 / D_MODEL) + EPS)
            for c in range(NCHUNK):
                cs = slice(c * LANES, (c + 1) * LANES)
                z = (y_scr[c, rs, :] - mu) * rstd * lng_ref[:, cs] + lnb_ref[:, cs]
                ybf_scr[rs, cs] = _silu(z).astype(BF16)
            return carry

        lax.fori_loop(0, tm // LN_ROWS, ln_block, 0)

    out = jnp.dot(ybf_scr[...], w2_ref[...], preferred_element_type=F32) + b2_ref[...]
    for s in range(nb):
        sl = slice(s * tt, (s + 1) * tt)
        o_ref[sl, :] = x_ref[sl, :] + gate_ref[s] * out[sl, :]


def _conv_pw2(g2, prev, x2, seq_len, mod, row0, w_dw, b_dw, ln_g, ln_b, w2, b2, tm=512, tn=512):
    rows = g2.shape[0]
    nb, tps = _row_tiling(seq_len, tm)
    tt = tm // nb
    hb = tm // HALO
    return pl.pallas_call(
        functools.partial(_conv_pw2_kernel, nb=nb, tps=tps),
        grid=(rows // tm, D_MODEL // tn),
        in_specs=[
            pl.BlockSpec((tm, D_MODEL), lambda i, j: (i, 0)),
            pl.BlockSpec((HALO, D_MODEL), lambda i, j: (jnp.maximum(i * hb - 1, 0), 0)),
            pl.BlockSpec((nb, HALO, D_MODEL), lambda i, j: (i // tps, 0, 0)),
            pl.BlockSpec((tm, tn), lambda i, j: (i, j)),
            _mod_spec(GATE1, row0, nb, tps, width=tn),
            pl.BlockSpec((NCHUNK, CONV_W, LANES), lambda i, j: (0, 0, 0)),
            pl.BlockSpec((NCHUNK, 1, LANES), lambda i, j: (0, 0, 0)),
            pl.BlockSpec((1, D_MODEL), lambda i, j: (0, 0)),
            pl.BlockSpec((1, D_MODEL), lambda i, j: (0, 0)),
            pl.BlockSpec((D_MODEL, tn), lambda i, j: (0, j)),
            pl.BlockSpec((1, tn), lambda i, j: (0, j)),
        ],
        out_specs=pl.BlockSpec((tm, tn), lambda i, j: (i, j)),
        out_shape=jax.ShapeDtypeStruct((rows, D_MODEL), F32),
        scratch_shapes=[
            pltpu.VMEM((nb * NCHUNK, HALO + tt, LANES), F32),
            pltpu.VMEM((NCHUNK, tm, LANES), F32),
            pltpu.VMEM((tm, D_MODEL), BF16),
        ],
        compiler_params=_params(("arbitrary", "arbitrary")),
        name="conv_pw2",
    )(g2, g2, prev, x2, mod,
      w_dw.reshape(CONV_W, NCHUNK, LANES).transpose(1, 0, 2), b_dw.reshape(NCHUNK, 1, LANES),
      ln_g, ln_b, w2, b2)


FFN_SUB = 512


def _ffn_kernel(x_ref, gain_ref, scale_ref, shift_ref, gate_ref, wa_ref, wb_ref, wo_ref, fin_ref,
                o_ref, h_scr, *, nb, final_norm):
    f = pl.program_id(1)
    tm = x_ref.shape[0]

    @pl.when(f == 0)
    def _():
        _fill_h(h_scr, x_ref, gain_ref, scale_ref, shift_ref, nb)
        o_ref[...] = jnp.zeros_like(o_ref)

    for m in range(tm // FFN_SUB):
        ms = slice(m * FFN_SUB, (m + 1) * FFN_SUB)
        h = h_scr[ms, :]
        a = jnp.dot(h, wa_ref[...], preferred_element_type=F32)
        b = jnp.dot(h, wb_ref[...], preferred_element_type=F32)
        act = (_silu(a) * b).astype(BF16)
        o_ref[ms, :] += jnp.dot(act, wo_ref[...], preferred_element_type=F32)

    @pl.when(f == pl.num_programs(1) - 1)
    def _():
        rows = tm // nb
        for s in range(nb):
            sl = slice(s * rows, (s + 1) * rows)
            y = x_ref[sl, :] + gate_ref[s] * o_ref[sl, :]
            if final_norm:
                y = y * lax.rsqrt(jnp.mean(y * y, axis=-1, keepdims=True) + EPS) * fin_ref[...]
            o_ref[sl, :] = y


def _ffn(x2, seq_len, gain, mod, row0, w_in, w_out, fin_gain, final_norm, tm=1024, tf=512):
    rows = x2.shape[0]
    nb, tps = _row_tiling(seq_len, tm)
    nf = D_FF // tf
    vec_spec = pl.BlockSpec((1, D_MODEL), lambda i, f: (0, 0))
    return pl.pallas_call(
        functools.partial(_ffn_kernel, nb=nb, final_norm=final_norm),
        grid=(rows // tm, nf),
        in_specs=[
            pl.BlockSpec((tm, D_MODEL), lambda i, f: (i, 0), pipeline_mode=pl.Buffered(1)),
            vec_spec,
            _mod_spec(SCALE2, row0, nb, tps), _mod_spec(SHIFT2, row0, nb, tps), _mod_spec(GATE2, row0, nb, tps),
            pl.BlockSpec((D_MODEL, tf), lambda i, f: (0, f)),
            pl.BlockSpec((D_MODEL, tf), lambda i, f: (0, nf + f)),
            pl.BlockSpec((tf, D_MODEL), lambda i, f: (f, 0)),
            vec_spec,
        ],
        out_specs=pl.BlockSpec((tm, D_MODEL), lambda i, f: (i, 0)),
        out_shape=jax.ShapeDtypeStruct((rows, D_MODEL), F32),
        scratch_shapes=[pltpu.VMEM((tm, D_MODEL), BF16)],
        compiler_params=_params(("arbitrary", "arbitrary")),
        name="ffn",
    )(x2, gain, mod, mod, mod, w_in, w_in, w_out, fin_gain)


def _qkv_kernel(x_ref, gain_ref, scale_ref, shift_ref, wq_ref, wkv_ref, q_ref, kv_ref, h_scr, *, nb):
    @pl.when(pl.program_id(1) == 0)
    def _():
        _fill_h(h_scr, x_ref, gain_ref, scale_ref, shift_ref, nb)
        kv_ref[...] = jnp.dot(h_scr[...], wkv_ref[...], preferred_element_type=F32)

    q = jnp.dot(h_scr[...], wq_ref[...], preferred_element_type=F32)
    q_ref[...] = (q * (HEAD_DIM ** -0.5)).astype(BF16)


def _qkv(x2, seq_len, gain, mod, row0, w_qkv, tm=1024, tn=512):
    rows = x2.shape[0]
    nb, tps = _row_tiling(seq_len, tm)
    nq = (N_HEADS * HEAD_DIM) // tn
    kv_cols = 2 * N_KV * HEAD_DIM
    assert kv_cols == tn
    return pl.pallas_call(
        functools.partial(_qkv_kernel, nb=nb),
        grid=(rows // tm, nq),
        in_specs=[
            pl.BlockSpec((tm, D_MODEL), lambda i, j: (i, 0)),
            pl.BlockSpec((1, D_MODEL), lambda i, j: (0, 0)),
            _mod_spec(SCALE1, row0, nb, tps), _mod_spec(SHIFT1, row0, nb, tps),
            pl.BlockSpec((D_MODEL, tn), lambda i, j: (0, j)),
            pl.BlockSpec((D_MODEL, kv_cols), lambda i, j: (0, nq)),
        ],
        out_specs=[
            pl.BlockSpec((tm, tn), lambda i, j: (i, j)),
            pl.BlockSpec((tm, kv_cols), lambda i, j: (i, 0)),
        ],
        out_shape=[
            jax.ShapeDtypeStruct((rows, N_HEADS * HEAD_DIM), BF16),
            jax.ShapeDtypeStruct((rows, kv_cols), F32),
        ],
        scratch_shapes=[pltpu.VMEM((tm, D_MODEL), BF16)],
        compiler_params=_params(("arbitrary", "arbitrary")),
        name="qkv",
    )(x2, gain, mod, mod, w_qkv, w_qkv)


PAIRS_PER_KV = GROUP // 2
QROWS = PAIRS_PER_KV * CHUNK
BAND = WINDOW + CHUNK
ATT_G = 8
KVW = N_KV * HEAD_DIM


def _attn_kernel(slope_ref, sink_ref, q_ref, *refs, band, steps_per_seq):
    if band:
        kp2_ref, kp1_ref, kc_ref, vp2_ref, vp1_ref, vc_ref = refs[:6]
        refs = refs[6:]
    else:
        ck_ref, kn_ref, cv_ref, vn_ref = refs[:4]
        refs = refs[4:]
    o_ref, bias_scr, ones_scr, kl_scr, kr_scr, vl_scr, vr_scr = refs
    step = pl.program_id(0)

    @pl.when(step == 0)
    def _():
        row = lax.broadcasted_iota(jnp.int32, (QROWS, 2 * KPAD), 0)
        col = lax.broadcasted_iota(jnp.int32, (QROWS, 2 * KPAD), 1)
        key = jnp.bitwise_and(col, KPAD - 1)
        dist = jnp.abs(WINDOW + jnp.bitwise_and(row, CHUNK - 1) - key).astype(F32)
        for kv in range(N_KV):
            slope = jnp.zeros((QROWS, 2 * KPAD), F32)
            sink = jnp.zeros((QROWS, 2 * KPAD), F32)
            for pr in range(PAIRS_PER_KV):
                in_pair = jnp.logical_and(row >= pr * CHUNK, row < (pr + 1) * CHUNK)
                in_a = jnp.logical_and(in_pair, col < KPAD)
                in_b = jnp.logical_and(in_pair, col >= KPAD)
                head = kv * GROUP + pr * 2
                slope = jnp.where(in_a, slope_ref[head], jnp.where(in_b, slope_ref[head + 1], slope))
                sink = jnp.where(in_a, sink_ref[head], jnp.where(in_b, sink_ref[head + 1], sink))
            bias_scr[kv] = jnp.where(key < BAND, -(slope * dist), jnp.where(key == BAND, sink, NEG))
        orow = lax.broadcasted_iota(jnp.int32, (2 * KPAD, PAIR), 0)
        ocol = lax.broadcasted_iota(jnp.int32, (2 * KPAD, PAIR), 1)
        in_first = ocol < HEAD_DIM
        ones_scr[...] = jnp.where(orow < KPAD, jnp.where(in_first, 1.0, 0.0),
                                  jnp.where(in_first, 0.0, 1.0)).astype(BF16)

    lane64 = lax.broadcasted_iota(jnp.int32, (CHUNK, PAIR), 1) < HEAD_DIM

    def prep(src_ref, r_src, col0, r_dst, l_scr, r_scr):
        for lb in range(N_KV // 2):
            blk = src_ref[r_src:r_src + CHUNK, col0 + lb * PAIR:col0 + (lb + 1) * PAIR]
            ev = jnp.where(lane64, blk, 0.0)
            od = jnp.where(lane64, 0.0, blk)
            ds = slice(r_dst, r_dst + CHUNK)
            l_scr[2 * lb, ds, :] = ev.astype(BF16)
            r_scr[2 * lb, ds, :] = pltpu.roll(ev, HEAD_DIM, axis=1).astype(BF16)
            r_scr[2 * lb + 1, ds, :] = od.astype(BF16)
            l_scr[2 * lb + 1, ds, :] = pltpu.roll(od, HEAD_DIM, axis=1).astype(BF16)

    if band:
        prep(kp2_ref, 0, 0, 0, kl_scr, kr_scr)
        prep(kp1_ref, 0, 0, CHUNK, kl_scr, kr_scr)
        prep(vp2_ref, 0, 0, 0, vl_scr, vr_scr)
        prep(vp1_ref, 0, 0, CHUNK, vl_scr, vr_scr)
        for g in range(ATT_G):
            prep(kc_ref, g * CHUNK, 0, WINDOW + g * CHUNK, kl_scr, kr_scr)
            prep(vc_ref, g * CHUNK, 0, WINDOW + g * CHUNK, vl_scr, vr_scr)
    else:
        for g in range(ATT_G):
            for h in range(WINDOW // CHUNK):
                prep(ck_ref, g * WINDOW + h * CHUNK, 0, g * BAND + h * CHUNK, kl_scr, kr_scr)
                prep(cv_ref, g * WINDOW + h * CHUNK, 0, g * BAND + h * CHUNK, vl_scr, vr_scr)
            prep(kn_ref, g * CHUNK, 0, g * BAND + WINDOW, kl_scr, kr_scr)
            prep(vn_ref, g * CHUNK, 0, g * BAND + WINDOW, vl_scr, vr_scr)

    zpad = jnp.zeros((KPAD - BAND, PAIR), BF16)
    keyrow = jnp.bitwise_and(lax.broadcasted_iota(jnp.int32, (1, 2 * KPAD), 1), KPAD - 1)
    n0 = (step % steps_per_seq) * ATT_G

    def chunk(c, carry, masked):
        r0 = pl.multiple_of(c * CHUNK, CHUNK)
        k0 = r0 if band else pl.multiple_of(c * BAND, CHUNK)
        ks = pl.ds(k0, BAND)
        qrows = pl.ds(r0, CHUNK)
        if masked:
            maskrow = jnp.where(keyrow >= WINDOW - (n0 + c) * CHUNK, 0.0, NEG)
        for kv in range(N_KV):
            qs = jnp.concatenate(
                [q_ref[qrows, kv * GROUP * HEAD_DIM + pr * PAIR: kv * GROUP * HEAD_DIM + (pr + 1) * PAIR]
                 for pr in range(PAIRS_PER_KV)], axis=0)
            kbd = jnp.concatenate([kl_scr[kv, ks, :], zpad, kr_scr[kv, ks, :], zpad], axis=0)
            vbd = jnp.concatenate([vl_scr[kv, ks, :], zpad, vr_scr[kv, ks, :], zpad], axis=0)
            s = lax.dot_general(qs, kbd, (((1,), (1,)), ((), ())), preferred_element_type=F32)
            s = s + bias_scr[kv]
            if masked:
                s = s + maskrow
            s_a = s[:, :KPAD]
            s_b = s[:, KPAD:]
            m_a = jnp.max(s_a, axis=-1, keepdims=True)
            m_b = jnp.max(s_b, axis=-1, keepdims=True)
            p = jnp.concatenate([jnp.exp(s_a - m_a), jnp.exp(s_b - m_b)], axis=1).astype(BF16)
            ol = jnp.dot(p, jnp.concatenate([vbd, ones_scr[...]], axis=1), preferred_element_type=F32)
            o = ol[:, :PAIR] / ol[:, PAIR:]
            for pr in range(PAIRS_PER_KV):
                c0 = kv * GROUP * HEAD_DIM + pr * PAIR
                o_ref[qrows, c0:c0 + PAIR] = o[pr * CHUNK:(pr + 1) * CHUNK, :].astype(BF16)
        return carry

    n_lead = WINDOW // CHUNK if band else 0
    if n_lead:
        lax.fori_loop(0, n_lead, functools.partial(chunk, masked=True), 0, unroll=2)
    lax.fori_loop(n_lead, ATT_G, functools.partial(chunk, masked=False), 0, unroll=2)


def _attention(q, kv, cache_k, cache_v, slopes, sinks, seq_len):
    rows = q.shape[0]
    tq = ATT_G * CHUNK
    band = cache_k is None
    smem = pl.BlockSpec(memory_space=pltpu.SMEM)
    if band:
        assert seq_len % tq == 0
        sps = seq_len // tq
        cps = seq_len // CHUNK

        def prev(d, col):
            return pl.BlockSpec((CHUNK, KVW), lambda i: (jnp.maximum((i % sps) * ATT_G - d, 0) + (i // sps) * cps, col))

        kv_specs = [prev(2, 0), prev(1, 0), pl.BlockSpec((tq, KVW), lambda i: (i, 0)),
                    prev(2, 1), prev(1, 1), pl.BlockSpec((tq, KVW), lambda i: (i, 1))]
        kv_args = [kv] * 6
        key_rows = WINDOW + tq
    else:
        assert seq_len == CHUNK
        sps = 1
        kv_specs = [pl.BlockSpec((ATT_G * WINDOW, KVW), lambda i: (i, 0)), pl.BlockSpec((tq, KVW), lambda i: (i, 0)),
                    pl.BlockSpec((ATT_G * WINDOW, KVW), lambda i: (i, 0)), pl.BlockSpec((tq, KVW), lambda i: (i, 1))]
        kv_args = [cache_k, kv, cache_v, kv]
        key_rows = ATT_G * BAND
    side = pltpu.VMEM((N_KV, key_rows, PAIR), BF16)
    return pl.pallas_call(
        functools.partial(_attn_kernel, band=band, steps_per_seq=sps),
        grid=(rows // tq,),
        in_specs=[smem, smem, pl.BlockSpec((tq, N_HEADS * HEAD_DIM), lambda i: (i, 0))] + kv_specs,
        out_specs=pl.BlockSpec((tq, N_HEADS * HEAD_DIM), lambda i: (i, 0)),
        out_shape=jax.ShapeDtypeStruct((rows, N_HEADS * HEAD_DIM), BF16),
        scratch_shapes=[
            pltpu.VMEM((N_KV, QROWS, 2 * KPAD), F32),
            pltpu.VMEM((2 * KPAD, PAIR), BF16),
            side, side, side, side,
        ],
        compiler_params=_params(("arbitrary",)),
        name="attention",
    )(slopes, sinks, q, *kv_args)


def _wo_kernel(a_ref, w_ref, x_ref, gate_ref, o_ref, *, nb):
    out = jnp.dot(a_ref[...], w_ref[...], preferred_element_type=F32)
    rows = a_ref.shape[0] // nb
    for s in range(nb):
        sl = slice(s * rows, (s + 1) * rows)
        o_ref[sl, :] = x_ref[sl, :] + gate_ref[s] * out[sl, :]


def _wo(attn, x2, seq_len, mod, row0, w_o, tm=1024, tn=1024):
    rows = x2.shape[0]
    nb, tps = _row_tiling(seq_len, tm)
    return pl.pallas_call(
        functools.partial(_wo_kernel, nb=nb),
        grid=(rows // tm, D_MODEL // tn),
        in_specs=[
            pl.BlockSpec((tm, N_HEADS * HEAD_DIM), lambda i, j: (i, 0)),
            pl.BlockSpec((N_HEADS * HEAD_DIM, tn), lambda i, j: (0, j)),
            pl.BlockSpec((tm, tn), lambda i, j: (i, j)),
            _mod_spec(GATE1, row0, nb, tps, width=tn),
        ],
        out_specs=pl.BlockSpec((tm, tn), lambda i, j: (i, j)),
        out_shape=jax.ShapeDtypeStruct((rows, D_MODEL), F32),
        compiler_params=_params(("arbitrary", "arbitrary")),
        name="wo",
    )(attn, w_o, x2, mod)


def _trunk(x, mods, row0, state_conv, cache_k, cache_v, wts):
    batch, seq_len, _ = x.shape
    rows = batch * seq_len
    x2 = x.reshape(rows, D_MODEL)
    prompt = state_conv is None
    row = lambda v: v.reshape(1, -1)
    assert seq_len >= CONV_W - 1 and seq_len >= WINDOW // 2

    g = _pw1_glu(x2, seq_len, row(wts["norm_mix_g"][0]), mods[0], row0, wts["w_pw1"], row(wts["b_pw1"][0]))
    if prompt:
        prev = jnp.zeros((batch, HALO, D_MODEL), F32)
    else:
        prev = jnp.pad(state_conv[0], ((0, 0), (HALO - (CONV_W - 1), 0), (0, 0)))
    conv_new = g.reshape(batch, seq_len, D_MODEL)[:, -(CONV_W - 1):][None]
    x2 = _conv_pw2(g, prev, x2, seq_len, mods[0], row0, wts["w_dw"][0], row(wts["b_dw"][0]), row(wts["ln_g"][0]),
                   row(wts["ln_b"][0]), wts["w_pw2"], row(wts["b_pw2"][0]))
    x2 = _ffn(x2, seq_len, row(wts["norm_ffn_g"][0]), mods[0], row0, wts["w_ffn_in"][0], wts["w_ffn_out"][0],
              row(wts["final_norm_g"]), final_norm=False)

    q, kv = _qkv(x2, seq_len, row(wts["norm_mix_g"][1]), mods[1], row0, wts["w_qkv"])
    kv3 = kv.reshape(batch, seq_len, 2 * KVW)
    if prompt:
        attn = _attention(q, kv, None, None, wts["slopes"], wts["sinks"], seq_len)
        k_new = kv3[:, -WINDOW:, :KVW]
        v_new = kv3[:, -WINDOW:, KVW:]
    else:
        ck = cache_k[0].reshape(batch * WINDOW, KVW)
        cv = cache_v[0].reshape(batch * WINDOW, KVW)
        attn = _attention(q, kv, ck, cv, wts["slopes"], wts["sinks"], seq_len)
        keep = WINDOW - seq_len
        k_new = jnp.concatenate([cache_k[0].reshape(batch, WINDOW, KVW)[:, -keep:], kv3[:, :, :KVW]], axis=1)
        v_new = jnp.concatenate([cache_v[0].reshape(batch, WINDOW, KVW)[:, -keep:], kv3[:, :, KVW:]], axis=1)
    x2 = _wo(attn, x2, seq_len, mods[1], row0, wts["w_o"])
    y2 = _ffn(x2, seq_len, row(wts["norm_ffn_g"][1]), mods[1], row0, wts["w_ffn_in"][1], wts["w_ffn_out"][1],
              row(wts["final_norm_g"]), final_norm=True)

    shape_kv = (1, batch, WINDOW, N_KV, HEAD_DIM)
    return (y2.reshape(batch, seq_len, D_MODEL), conv_new,
            k_new.reshape(shape_kv), v_new.reshape(shape_kv))


def kernel(x_prompt, x_sample, c_prompt, c_sample, state_conv, cache_k, cache_v, norm_mix_g, norm_ffn_g,
           w_ada, b_ada, w_pw1, b_pw1, w_dw, b_dw, ln_g, ln_b, w_pw2, b_pw2, w_qkv, w_o, sinks,
           w_ffn_in, w_ffn_out, final_norm_g):
    bp, bs = c_prompt.shape[0], c_sample.shape[0]
    b_pad = -(-(bs + bp) // 8) * 8
    c_all = jnp.concatenate([c_sample, c_prompt, jnp.zeros((b_pad - bp - bs, D_MODEL), F32)], axis=0)
    mod = _adaln(c_all, w_ada, b_ada)
    mods = [mod[l].reshape(b_pad, 1, 6 * D_MODEL) for l in range(DEPTH)]

    heads = jnp.arange(1, N_HEADS + 1, dtype=F32)
    wts = dict(
        norm_mix_g=norm_mix_g, norm_ffn_g=norm_ffn_g, final_norm_g=final_norm_g,
        w_pw1=w_pw1[0].astype(BF16), b_pw1=b_pw1, w_dw=w_dw, b_dw=b_dw, ln_g=ln_g, ln_b=ln_b,
        w_pw2=w_pw2[0].astype(BF16), b_pw2=b_pw2, w_qkv=w_qkv[0].astype(BF16), w_o=w_o[0].astype(BF16),
        w_ffn_in=[w_ffn_in[l].astype(BF16) for l in range(DEPTH)],
        w_ffn_out=[w_ffn_out[l].astype(BF16) for l in range(DEPTH)],
        slopes=jnp.exp2(-8.0 * heads / N_HEADS), sinks=sinks[0].astype(F32),
    )
    y_p, conv_p, k_p, v_p = _trunk(x_prompt, mods, bs, None, None, None, wts)
    y_s, conv_s, k_s, v_s = _trunk(x_sample, mods, 0, state_conv, cache_k, cache_v, wts)
    return (y_p, y_s, conv_p, conv_s, k_p, v_p, k_s, v_s)
```

```python
import functools

import jax
import jax.numpy as jnp
from jax import lax
from jax.experimental import pallas as pl
from jax.experimental.pallas import tpu as pltpu

F32 = jnp.float32
BF16 = jnp.bfloat16

D_MODEL = 2048
DEPTH = 2
CHUNK = 64
CONV_W = 31
N_HEADS = 32
N_KV = 4
HEAD_DIM = 64
GROUP = N_HEADS // N_KV
WINDOW = 128
D_FF = 5632
EPS = 1e-6
NEG = -1e30

VMEM_LIMIT_BYTES = 60 * 1024 * 1024
LANES = 128
HALO = 32
PAIR = 2 * HEAD_DIM
KPAD = 256

SHIFT1, SCALE1, GATE1, SHIFT2, SCALE2, GATE2 = range(6)


def _params(sem):
    return pltpu.CompilerParams(dimension_semantics=sem, vmem_limit_bytes=VMEM_LIMIT_BYTES)


def _silu(v):
    return v * jax.nn.sigmoid(v)


FILL_ROWS = 128


def _fill_h(h_scr, x_ref, gain_ref, scale_ref, shift_ref, nb):
    rows = x_ref.shape[0] // nb
    blk = min(rows, FILL_ROWS)
    for s in range(nb):
        mul = gain_ref[...] * (1.0 + scale_ref[s])
        add = shift_ref[s]

        def body(r, carry, s=s, mul=mul, add=add):
            rs = pl.ds(pl.multiple_of(s * rows + r * blk, blk), blk)
            x = x_ref[rs, :]
            y = x * lax.rsqrt(jnp.mean(x * x, axis=-1, keepdims=True) + EPS)
            h_scr[rs, :] = (y * mul + add).astype(BF16)
            return carry

        lax.fori_loop(0, rows // blk, body, 0)


def _adaln_kernel(c_ref, w_ref, b_ref, o_ref):
    c = c_ref[...]
    a = _silu(c).astype(BF16)
    o_ref[0] = jnp.dot(a, w_ref[0].astype(BF16), preferred_element_type=F32) + b_ref[0]


def _adaln(c_all, w_ada, b_ada, tn=1024):
    bp = c_all.shape[0]
    n = w_ada.shape[-1]
    return pl.pallas_call(
        _adaln_kernel,
        grid=(DEPTH, n // tn),
        in_specs=[
            pl.BlockSpec((bp, D_MODEL), lambda l, j: (0, 0)),
            pl.BlockSpec((1, D_MODEL, tn), lambda l, j: (l, 0, j)),
            pl.BlockSpec((1, 1, tn), lambda l, j: (l, 0, j)),
        ],
        out_specs=pl.BlockSpec((1, bp, tn), lambda l, j: (l, 0, j)),
        out_shape=jax.ShapeDtypeStruct((DEPTH, bp, n), F32),
        compiler_params=_params(("arbitrary", "arbitrary")),
        name="adaln",
    )(c_all, w_ada, b_ada.reshape(DEPTH, 1, n))


def _row_tiling(seq_len, tm):
    if tm >= seq_len:
        assert tm % seq_len == 0
        return tm // seq_len, 1
    assert seq_len % tm == 0
    return 1, seq_len // tm


def _mod_spec(which, row0, nb, tps, width=D_MODEL):
    assert row0 % nb == 0
    per = D_MODEL // width
    if per == 1:
        return pl.BlockSpec((nb, 1, width), lambda i, j: (row0 // nb + i // tps, 0, which))
    return pl.BlockSpec((nb, 1, width), lambda i, j: (row0 // nb + i // tps, 0, which * per + j))


def _pw1_glu_kernel(x_ref, gain_ref, scale_ref, shift_ref, wa_ref, wb_ref, ba_ref, bb_ref,
                    g_ref, h_scr, *, nb):
    @pl.when(pl.program_id(1) == 0)
    def _():
        _fill_h(h_scr, x_ref, gain_ref, scale_ref, shift_ref, nb)

    h = h_scr[...]
    a = jnp.dot(h, wa_ref[...], preferred_element_type=F32) + ba_ref[...]
    b = jnp.dot(h, wb_ref[...], preferred_element_type=F32) + bb_ref[...]
    g_ref[...] = a * jax.nn.sigmoid(b)


def _pw1_glu(x2, seq_len, gain, mod, row0, w1, b1, tm=1024, tn=512):
    rows = x2.shape[0]
    nb, tps = _row_tiling(seq_len, tm)
    nj = D_MODEL // tn
    return pl.pallas_call(
        functools.partial(_pw1_glu_kernel, nb=nb),
        grid=(rows // tm, nj),
        in_specs=[
            pl.BlockSpec((tm, D_MODEL), lambda i, j: (i, 0)),
            pl.BlockSpec((1, D_MODEL), lambda i, j: (0, 0)),
            _mod_spec(SCALE1, row0, nb, tps), _mod_spec(SHIFT1, row0, nb, tps),
            pl.BlockSpec((D_MODEL, tn), lambda i, j: (0, j)),
            pl.BlockSpec((D_MODEL, tn), lambda i, j: (0, nj + j)),
            pl.BlockSpec((1, tn), lambda i, j: (0, j)),
            pl.BlockSpec((1, tn), lambda i, j: (0, nj + j)),
        ],
        out_specs=pl.BlockSpec((tm, tn), lambda i, j: (i, j)),
        out_shape=jax.ShapeDtypeStruct((rows, D_MODEL), F32),
        scratch_shapes=[pltpu.VMEM((tm, D_MODEL), BF16)],
        compiler_params=_params(("arbitrary", "arbitrary")),
        name="pw1_glu",
    )(x2, gain, mod, mod, w1, w1, b1, b1)


CONV_ROWS = 64
LN_ROWS = 128
NCHUNK = D_MODEL // LANES


def _conv_pw2_kernel(g_ref, halo_ref, prev_ref, x_ref, gate_ref, wdw_ref, bdw_ref, lng_ref, lnb_ref,
                     w2_ref, b2_ref, *rest, nb, tps, n_cast):
    cast_src = rest[:n_cast]
    o_ref = rest[n_cast]
    cast_dst = rest[n_cast + 1:2 * n_cast + 1]
    win_scr, y_scr, ybf_scr = rest[2 * n_cast + 1:]
    i = pl.program_id(0)
    j = pl.program_id(1)
    nj = pl.num_programs(1)
    tm = g_ref.shape[0]
    tt = tm // nb
    ti = jnp.minimum(i, pl.num_programs(0) - 2)
    cur = i % 2
    chunks_per_step = w2_ref.shape[1] // LANES

    for src, dst in zip(cast_src, cast_dst):
        dst[...] = src[...].astype(BF16)

    @pl.when(j == 0)
    def _():
        @pl.when(i == 0)
        def _():
            ybf_scr[1] = jnp.zeros(ybf_scr.shape[1:], BF16)

        def put_history(src_of):
            for s in range(nb):
                for c in range(NCHUNK):
                    win_scr[s * NCHUNK + c, 0:HALO, :] = src_of(s, c)

        if tps == 1:
            put_history(lambda s, c: prev_ref[s, :, c * LANES:(c + 1) * LANES])
        else:
            first = (ti % tps) == 0

            @pl.when(first)
            def _():
                put_history(lambda s, c: prev_ref[s, :, c * LANES:(c + 1) * LANES])

            @pl.when(jnp.logical_not(first))
            def _():
                put_history(lambda s, c: halo_ref[:, c * LANES:(c + 1) * LANES])
        for s in range(nb):
            for c in range(NCHUNK):
                win_scr[s * NCHUNK + c, HALO:HALO + tt, :] = g_ref[s * tt:(s + 1) * tt, c * LANES:(c + 1) * LANES]

    lead = HALO - (CONV_W - 1)
    for cc in range(chunks_per_step):
        c = j * chunks_per_step + cc
        for s in range(nb):
            for rb in range(tt // CONV_ROWS):
                acc = jnp.zeros((CONV_ROWS, LANES), F32) + bdw_ref[c]
                for k in range(CONV_W):
                    r0 = rb * CONV_ROWS + k + lead
                    acc = acc + win_scr[s * NCHUNK + c, r0:r0 + CONV_ROWS, :] * wdw_ref[c, k:k + 1, :]
                y_scr[c, s * tt + rb * CONV_ROWS:s * tt + (rb + 1) * CONV_ROWS, :] = acc

    out = jnp.dot(ybf_scr[1 - cur], w2_ref[...], preferred_element_type=F32) + b2_ref[...]
    for s in range(nb):
        sl = slice(s * tt, (s + 1) * tt)
        o_ref[sl, :] = x_ref[sl, :] + gate_ref[s] * out[sl, :]

    @pl.when(j == nj - 1)
    def _():
        def ln_block(r, carry):
            rs = pl.ds(pl.multiple_of(r * LN_ROWS, LN_ROWS), LN_ROWS)
            tot = y_scr[0, rs, :]
            for c in range(1, NCHUNK):
                tot = tot + y_scr[c, rs, :]
            mu = jnp.sum(tot, axis=-1, keepdims=True) * (1.0 / D_MODEL)
            sq = jnp.zeros((LN_ROWS, LANES), F32)
            for c in range(NCHUNK):
                yc = y_scr[c, rs, :] - mu
                sq = sq + yc * yc
            rstd = lax.rsqrt(jnp.sum(sq, axis=-1, keepdims=True) * (1.0 / D_MODEL) + EPS)
            for c in range(NCHUNK):
                cs = slice(c * LANES, (c + 1) * LANES)
                z = (y_scr[c, rs, :] - mu) * rstd * lng_ref[:, cs] + lnb_ref[:, cs]
                ybf_scr[cur, rs, cs] = _silu(z).astype(BF16)
            return carry

        lax.fori_loop(0, tm // LN_ROWS, ln_block, 0)


def _conv_pw2(g2, prev, x2, seq_len, mod, row0, w_dw, b_dw, ln_g, ln_b, w2, b2, casts=(), tm=512, tn=512):
    rows = g2.shape[0]
    nb, tps = _row_tiling(seq_len, tm)
    tt = tm // nb
    hb = tm // HALO
    nj = D_MODEL // tn
    n_tiles = rows // tm
    n_steps = n_tiles * nj
    cast_specs = []
    for a in casts:
        slab = a.shape[0] // n_steps
        assert a.shape[0] % n_steps == 0 and slab % 16 == 0, a.shape
        cast_specs.append(pl.BlockSpec((slab, a.shape[1]), lambda i, j: (jnp.minimum(i * nj + j, n_steps - 1), 0)))
    front = lambda i: jnp.minimum(i, n_tiles - 1)
    back = lambda i: jnp.maximum(i - 1, 0)
    assert row0 % nb == 0
    outs = pl.pallas_call(
        functools.partial(_conv_pw2_kernel, nb=nb, tps=tps, n_cast=len(casts)),
        grid=(n_tiles + 1, nj),
        in_specs=[
            pl.BlockSpec((tm, D_MODEL), lambda i, j: (front(i), 0)),
            pl.BlockSpec((HALO, D_MODEL), lambda i, j: (jnp.maximum(front(i) * hb - 1, 0), 0)),
            pl.BlockSpec((nb, HALO, D_MODEL), lambda i, j: (front(i) // tps, 0, 0)),
            pl.BlockSpec((tm, tn), lambda i, j: (back(i), j)),
            pl.BlockSpec((nb, 1, tn), lambda i, j: (row0 // nb + back(i) // tps, 0, GATE1 * nj + j)),
            pl.BlockSpec((NCHUNK, CONV_W, LANES), lambda i, j: (0, 0, 0)),
            pl.BlockSpec((NCHUNK, 1, LANES), lambda i, j: (0, 0, 0)),
            pl.BlockSpec((1, D_MODEL), lambda i, j: (0, 0)),
            pl.BlockSpec((1, D_MODEL), lambda i, j: (0, 0)),
            pl.BlockSpec((D_MODEL, tn), lambda i, j: (0, j)),
            pl.BlockSpec((1, tn), lambda i, j: (0, j)),
        ] + cast_specs,
        out_specs=[pl.BlockSpec((tm, tn), lambda i, j: (back(i), j * jnp.minimum(i, 1)))] + cast_specs,
        out_shape=[jax.ShapeDtypeStruct((rows, D_MODEL), F32)]
        + [jax.ShapeDtypeStruct(a.shape, BF16) for a in casts],
        scratch_shapes=[
            pltpu.VMEM((nb * NCHUNK, HALO + tt, LANES), F32),
            pltpu.VMEM((NCHUNK, tm, LANES), F32),
            pltpu.VMEM((2, tm, D_MODEL), BF16),
        ],
        compiler_params=_params(("arbitrary", "arbitrary")),
        name="conv_pw2",
    )(g2, g2, prev, x2, mod,
      w_dw.reshape(CONV_W, NCHUNK, LANES).transpose(1, 0, 2), b_dw.reshape(NCHUNK, 1, LANES),
      ln_g, ln_b, w2, b2, *casts)
    return outs[0], list(outs[1:])


FFN_SUB = 512


def _ffn_kernel(x_ref, gain_ref, scale_ref, shift_ref, gate_ref, wa_ref, wb_ref, wo_ref, fin_ref,
                o_ref, h_scr, *, nb, final_norm):
    f = pl.program_id(1)
    tm = x_ref.shape[0]

    @pl.when(f == 0)
    def _():
        _fill_h(h_scr, x_ref, gain_ref, scale_ref, shift_ref, nb)
        o_ref[...] = jnp.zeros_like(o_ref)

    for m in range(tm // FFN_SUB):
        ms = slice(m * FFN_SUB, (m + 1) * FFN_SUB)
        h = h_scr[ms, :]
        a = jnp.dot(h, wa_ref[...], preferred_element_type=F32)
        b = jnp.dot(h, wb_ref[...], preferred_element_type=F32)
        act = (_silu(a) * b).astype(BF16)
        o_ref[ms, :] += jnp.dot(act, wo_ref[...], preferred_element_type=F32)

    @pl.when(f == pl.num_programs(1) - 1)
    def _():
        rows = tm // nb
        for s in range(nb):
            sl = slice(s * rows, (s + 1) * rows)
            y = x_ref[sl, :] + gate_ref[s] * o_ref[sl, :]
            if final_norm:
                y = y * lax.rsqrt(jnp.mean(y * y, axis=-1, keepdims=True) + EPS) * fin_ref[...]
            o_ref[sl, :] = y


def _ffn(x2, seq_len, gain, mod, row0, w_in, w_out, layer, fin_gain, final_norm, tm=1024, tf=512):
    rows = x2.shape[0]
    nb, tps = _row_tiling(seq_len, tm)
    nf = D_FF // tf
    vec_spec = pl.BlockSpec((1, D_MODEL), lambda i, f: (0, 0))
    return pl.pallas_call(
        functools.partial(_ffn_kernel, nb=nb, final_norm=final_norm),
        grid=(rows // tm, nf),
        in_specs=[
            pl.BlockSpec((tm, D_MODEL), lambda i, f: (i, 0), pipeline_mode=pl.Buffered(1)),
            vec_spec,
            _mod_spec(SCALE2, row0, nb, tps), _mod_spec(SHIFT2, row0, nb, tps), _mod_spec(GATE2, row0, nb, tps),
            pl.BlockSpec((None, D_MODEL, tf), lambda i, f: (layer, 0, f)),
            pl.BlockSpec((None, D_MODEL, tf), lambda i, f: (layer, 0, nf + f)),
            pl.BlockSpec((None, tf, D_MODEL), lambda i, f: (layer, f, 0)),
            vec_spec,
        ],
        out_specs=pl.BlockSpec((tm, D_MODEL), lambda i, f: (i, 0)),
        out_shape=jax.ShapeDtypeStruct((rows, D_MODEL), F32),
        scratch_shapes=[pltpu.VMEM((tm, D_MODEL), BF16)],
        compiler_params=_params(("arbitrary", "arbitrary")),
        name="ffn",
    )(x2, gain, mod, mod, mod, w_in, w_in, w_out, fin_gain)


def _qkv_kernel(x_ref, gain_ref, scale_ref, shift_ref, wq_ref, wkv_ref, q_ref, kv_ref, h_scr, *, nb):
    @pl.when(pl.program_id(1) == 0)
    def _():
        _fill_h(h_scr, x_ref, gain_ref, scale_ref, shift_ref, nb)
        kv_ref[...] = jnp.dot(h_scr[...], wkv_ref[...], preferred_element_type=F32)

    q = jnp.dot(h_scr[...], wq_ref[...], preferred_element_type=F32)
    q_ref[...] = (q * (HEAD_DIM ** -0.5)).astype(BF16)


def _qkv(x2, seq_len, gain, mod, row0, w_qkv, tm=1024, tn=512):
    rows = x2.shape[0]
    nb, tps = _row_tiling(seq_len, tm)
    nq = (N_HEADS * HEAD_DIM) // tn
    kv_cols = 2 * N_KV * HEAD_DIM
    assert kv_cols == tn
    return pl.pallas_call(
        functools.partial(_qkv_kernel, nb=nb),
        grid=(rows // tm, nq),
        in_specs=[
            pl.BlockSpec((tm, D_MODEL), lambda i, j: (i, 0)),
            pl.BlockSpec((1, D_MODEL), lambda i, j: (0, 0)),
            _mod_spec(SCALE1, row0, nb, tps), _mod_spec(SHIFT1, row0, nb, tps),
            pl.BlockSpec((D_MODEL, tn), lambda i, j: (0, j)),
            pl.BlockSpec((D_MODEL, kv_cols), lambda i, j: (0, nq)),
        ],
        out_specs=[
            pl.BlockSpec((tm, tn), lambda i, j: (i, j)),
            pl.BlockSpec((tm, kv_cols), lambda i, j: (i, 0)),
        ],
        out_shape=[
            jax.ShapeDtypeStruct((rows, N_HEADS * HEAD_DIM), BF16),
            jax.ShapeDtypeStruct((rows, kv_cols), F32),
        ],
        scratch_shapes=[pltpu.VMEM((tm, D_MODEL), BF16)],
        compiler_params=_params(("arbitrary", "arbitrary")),
        name="qkv",
    )(x2, gain, mod, mod, w_qkv, w_qkv)


PAIRS_PER_KV = GROUP // 2
QROWS = PAIRS_PER_KV * CHUNK
BAND = WINDOW + CHUNK
ATT_G = 8
KVW = N_KV * HEAD_DIM


def _attn_kernel(slope_ref, sink_ref, q_ref, *refs, band, steps_per_seq):
    if band:
        kp2_ref, kp1_ref, kc_ref, vp2_ref, vp1_ref, vc_ref = refs[:6]
        refs = refs[6:]
    else:
        ck_ref, kn_ref, cv_ref, vn_ref = refs[:4]
        refs = refs[4:]
    o_ref, bias_scr, ones_scr, kl_scr, kr_scr, vl_scr, vr_scr = refs
    step = pl.program_id(0)

    @pl.when(step == 0)
    def _():
        row = lax.broadcasted_iota(jnp.int32, (QROWS, 2 * KPAD), 0)
        col = lax.broadcasted_iota(jnp.int32, (QROWS, 2 * KPAD), 1)
        key = jnp.bitwise_and(col, KPAD - 1)
        dist = jnp.abs(WINDOW + jnp.bitwise_and(row, CHUNK - 1) - key).astype(F32)
        for kv in range(N_KV):
            slope = jnp.zeros((QROWS, 2 * KPAD), F32)
            sink = jnp.zeros((QROWS, 2 * KPAD), F32)
            for pr in range(PAIRS_PER_KV):
                in_pair = jnp.logical_and(row >= pr * CHUNK, row < (pr + 1) * CHUNK)
                in_a = jnp.logical_and(in_pair, col < KPAD)
                in_b = jnp.logical_and(in_pair, col >= KPAD)
                head = kv * GROUP + pr * 2
                slope = jnp.where(in_a, slope_ref[head], jnp.where(in_b, slope_ref[head + 1], slope))
                sink = jnp.where(in_a, sink_ref[head], jnp.where(in_b, sink_ref[head + 1], sink))
            bias_scr[kv] = jnp.where(key < BAND, -(slope * dist), jnp.where(key == BAND, sink, NEG))
        orow = lax.broadcasted_iota(jnp.int32, (2 * KPAD, PAIR), 0)
        ocol = lax.broadcasted_iota(jnp.int32, (2 * KPAD, PAIR), 1)
        in_first = ocol < HEAD_DIM
        ones_scr[...] = jnp.where(orow < KPAD, jnp.where(in_first, 1.0, 0.0),
                                  jnp.where(in_first, 0.0, 1.0)).astype(BF16)

    lane64 = lax.broadcasted_iota(jnp.int32, (CHUNK, PAIR), 1) < HEAD_DIM

    def prep(src_ref, r_src, col0, r_dst, l_scr, r_scr):
        for lb in range(N_KV // 2):
            blk = src_ref[r_src:r_src + CHUNK, col0 + lb * PAIR:col0 + (lb + 1) * PAIR]
            ev = jnp.where(lane64, blk, 0.0)
            od = jnp.where(lane64, 0.0, blk)
            ds = slice(r_dst, r_dst + CHUNK)
            l_scr[2 * lb, ds, :] = ev.astype(BF16)
            r_scr[2 * lb, ds, :] = pltpu.roll(ev, HEAD_DIM, axis=1).astype(BF16)
            r_scr[2 * lb + 1, ds, :] = od.astype(BF16)
            l_scr[2 * lb + 1, ds, :] = pltpu.roll(od, HEAD_DIM, axis=1).astype(BF16)

    if band:
        prep(kp2_ref, 0, 0, 0, kl_scr, kr_scr)
        prep(kp1_ref, 0, 0, CHUNK, kl_scr, kr_scr)
        prep(vp2_ref, 0, 0, 0, vl_scr, vr_scr)
        prep(vp1_ref, 0, 0, CHUNK, vl_scr, vr_scr)
        for g in range(ATT_G):
            prep(kc_ref, g * CHUNK, 0, WINDOW + g * CHUNK, kl_scr, kr_scr)
            prep(vc_ref, g * CHUNK, 0, WINDOW + g * CHUNK, vl_scr, vr_scr)
    else:
        for g in range(ATT_G):
            for h in range(WINDOW // CHUNK):
                prep(ck_ref, g * WINDOW + h * CHUNK, 0, g * BAND + h * CHUNK, kl_scr, kr_scr)
                prep(cv_ref, g * WINDOW + h * CHUNK, 0, g * BAND + h * CHUNK, vl_scr, vr_scr)
            prep(kn_ref, g * CHUNK, 0, g * BAND + WINDOW, kl_scr, kr_scr)
            prep(vn_ref, g * CHUNK, 0, g * BAND + WINDOW, vl_scr, vr_scr)

    zpad = jnp.zeros((KPAD - BAND, PAIR), BF16)
    keyrow = jnp.bitwise_and(lax.broadcasted_iota(jnp.int32, (1, 2 * KPAD), 1), KPAD - 1)
    n0 = (step % steps_per_seq) * ATT_G

    def chunk(c, carry, masked):
        r0 = pl.multiple_of(c * CHUNK, CHUNK)
        k0 = r0 if band else pl.multiple_of(c * BAND, CHUNK)
        ks = pl.ds(k0, BAND)
        qrows = pl.ds(r0, CHUNK)
        if masked:
            maskrow = jnp.where(keyrow >= WINDOW - (n0 + c) * CHUNK, 0.0, NEG)
        for kv in range(N_KV):
            qs = jnp.concatenate(
                [q_ref[qrows, kv * GROUP * HEAD_DIM + pr * PAIR: kv * GROUP * HEAD_DIM + (pr + 1) * PAIR]
                 for pr in range(PAIRS_PER_KV)], axis=0)
            kbd = jnp.concatenate([kl_scr[kv, ks, :], zpad, kr_scr[kv, ks, :], zpad], axis=0)
            vbd = jnp.concatenate([vl_scr[kv, ks, :], zpad, vr_scr[kv, ks, :], zpad], axis=0)
            s = lax.dot_general(qs, kbd, (((1,), (1,)), ((), ())), preferred_element_type=F32)
            s = s + bias_scr[kv]
            if masked:
                s = s + maskrow
            s_a = s[:, :KPAD]
            s_b = s[:, KPAD:]
            m_a = jnp.max(s_a, axis=-1, keepdims=True)
            m_b = jnp.max(s_b, axis=-1, keepdims=True)
            p = jnp.concatenate([jnp.exp(s_a - m_a), jnp.exp(s_b - m_b)], axis=1).astype(BF16)
            ol = jnp.dot(p, jnp.concatenate([vbd, ones_scr[...]], axis=1), preferred_element_type=F32)
            o = ol[:, :PAIR] / ol[:, PAIR:]
            for pr in range(PAIRS_PER_KV):
                c0 = kv * GROUP * HEAD_DIM + pr * PAIR
                o_ref[qrows, c0:c0 + PAIR] = o[pr * CHUNK:(pr + 1) * CHUNK, :].astype(BF16)
        return carry

    n_lead = WINDOW // CHUNK if band else 0
    if n_lead:
        lax.fori_loop(0, n_lead, functools.partial(chunk, masked=True), 0, unroll=2)
    lax.fori_loop(n_lead, ATT_G, functools.partial(chunk, masked=False), 0, unroll=2)


def _attention(q, kv, cache_k, cache_v, slopes, sinks, seq_len):
    rows = q.shape[0]
    tq = ATT_G * CHUNK
    band = cache_k is None
    smem = pl.BlockSpec(memory_space=pltpu.SMEM)
    if band:
        assert seq_len % tq == 0
        sps = seq_len // tq
        cps = seq_len // CHUNK

        def prev(d, col):
            return pl.BlockSpec((CHUNK, KVW), lambda i: (jnp.maximum((i % sps) * ATT_G - d, 0) + (i // sps) * cps, col))

        kv_specs = [prev(2, 0), prev(1, 0), pl.BlockSpec((tq, KVW), lambda i: (i, 0)),
                    prev(2, 1), prev(1, 1), pl.BlockSpec((tq, KVW), lambda i: (i, 1))]
        kv_args = [kv] * 6
        key_rows = WINDOW + tq
    else:
        assert seq_len == CHUNK
        sps = 1
        kv_specs = [pl.BlockSpec((ATT_G * WINDOW, KVW), lambda i: (i, 0)), pl.BlockSpec((tq, KVW), lambda i: (i, 0)),
                    pl.BlockSpec((ATT_G * WINDOW, KVW), lambda i: (i, 0)), pl.BlockSpec((tq, KVW), lambda i: (i, 1))]
        kv_args = [cache_k, kv, cache_v, kv]
        key_rows = ATT_G * BAND
    side = pltpu.VMEM((N_KV, key_rows, PAIR), BF16)
    return pl.pallas_call(
        functools.partial(_attn_kernel, band=band, steps_per_seq=sps),
        grid=(rows // tq,),
        in_specs=[smem, smem, pl.BlockSpec((tq, N_HEADS * HEAD_DIM), lambda i: (i, 0))] + kv_specs,
        out_specs=pl.BlockSpec((tq, N_HEADS * HEAD_DIM), lambda i: (i, 0)),
        out_shape=jax.ShapeDtypeStruct((rows, N_HEADS * HEAD_DIM), BF16),
        scratch_shapes=[
            pltpu.VMEM((N_KV, QROWS, 2 * KPAD), F32),
            pltpu.VMEM((2 * KPAD, PAIR), BF16),
            side, side, side, side,
        ],
        compiler_params=_params(("arbitrary",)),
        name="attention",
    )(slopes, sinks, q, *kv_args)


def _wo_kernel(a_ref, w_ref, x_ref, gate_ref, o_ref, *, nb):
    out = jnp.dot(a_ref[...], w_ref[...], preferred_element_type=F32)
    rows = a_ref.shape[0] // nb
    for s in range(nb):
        sl = slice(s * rows, (s + 1) * rows)
        o_ref[sl, :] = x_ref[sl, :] + gate_ref[s] * out[sl, :]


def _wo(attn, x2, seq_len, mod, row0, w_o, tm=1024, tn=1024):
    rows = x2.shape[0]
    nb, tps = _row_tiling(seq_len, tm)
    return pl.pallas_call(
        functools.partial(_wo_kernel, nb=nb),
        grid=(rows // tm, D_MODEL // tn),
        in_specs=[
            pl.BlockSpec((tm, N_HEADS * HEAD_DIM), lambda i, j: (i, 0)),
            pl.BlockSpec((N_HEADS * HEAD_DIM, tn), lambda i, j: (0, j)),
            pl.BlockSpec((tm, tn), lambda i, j: (i, j)),
            _mod_spec(GATE1, row0, nb, tps, width=tn),
        ],
        out_specs=pl.BlockSpec((tm, tn), lambda i, j: (i, j)),
        out_shape=jax.ShapeDtypeStruct((rows, D_MODEL), F32),
        compiler_params=_params(("arbitrary", "arbitrary")),
        name="wo",
    )(attn, w_o, x2, mod)


def _trunk(x, mods, row0, state_conv, cache_k, cache_v, wts, late):
    batch, seq_len, _ = x.shape
    rows = batch * seq_len
    x2 = x.reshape(rows, D_MODEL)
    prompt = state_conv is None
    row = lambda v: v.reshape(1, -1)
    assert seq_len >= CONV_W - 1 and seq_len >= WINDOW // 2

    g = _pw1_glu(x2, seq_len, row(wts["norm_mix_g"][0]), mods[0], row0, wts["w_pw1"], row(wts["b_pw1"][0]))
    if prompt:
        prev = jnp.zeros((batch, HALO, D_MODEL), F32)
    else:
        prev = jnp.pad(state_conv[0], ((0, 0), (HALO - (CONV_W - 1), 0), (0, 0)))
    conv_new = g.reshape(batch, seq_len, D_MODEL)[:, -(CONV_W - 1):][None]
    casts = ()
    if late is None:
        casts = (wts["w_ffn_in"].reshape(DEPTH * D_MODEL, 2 * D_FF), wts["w_ffn_out"].reshape(DEPTH * D_FF, D_MODEL),
                 wts["w_qkv"][0], wts["w_o"][0])
    x2, cast_out = _conv_pw2(g, prev, x2, seq_len, mods[0], row0, wts["w_dw"][0], row(wts["b_dw"][0]),
                             row(wts["ln_g"][0]), row(wts["ln_b"][0]), wts["w_pw2"], row(wts["b_pw2"][0]), casts)
    if late is None:
        late = dict(w_ffn_in=cast_out[0].reshape(DEPTH, D_MODEL, 2 * D_FF),
                    w_ffn_out=cast_out[1].reshape(DEPTH, D_FF, D_MODEL), w_qkv=cast_out[2], w_o=cast_out[3])
    x2 = _ffn(x2, seq_len, row(wts["norm_ffn_g"][0]), mods[0], row0, late["w_ffn_in"], late["w_ffn_out"], 0,
              row(wts["final_norm_g"]), final_norm=False)

    q, kv = _qkv(x2, seq_len, row(wts["norm_mix_g"][1]), mods[1], row0, late["w_qkv"])
    kv3 = kv.reshape(batch, seq_len, 2 * KVW)
    if prompt:
        attn = _attention(q, kv, None, None, wts["slopes"], wts["sinks"], seq_len)
        k_new = kv3[:, -WINDOW:, :KVW]
        v_new = kv3[:, -WINDOW:, KVW:]
    else:
        ck = cache_k[0].reshape(batch * WINDOW, KVW)
        cv = cache_v[0].reshape(batch * WINDOW, KVW)
        attn = _attention(q, kv, ck, cv, wts["slopes"], wts["sinks"], seq_len)
        keep = WINDOW - seq_len
        k_new = jnp.concatenate([cache_k[0].reshape(batch, WINDOW, KVW)[:, -keep:], kv3[:, :, :KVW]], axis=1)
        v_new = jnp.concatenate([cache_v[0].reshape(batch, WINDOW, KVW)[:, -keep:], kv3[:, :, KVW:]], axis=1)
    x2 = _wo(attn, x2, seq_len, mods[1], row0, late["w_o"])
    y2 = _ffn(x2, seq_len, row(wts["norm_ffn_g"][1]), mods[1], row0, late["w_ffn_in"], late["w_ffn_out"], 1,
              row(wts["final_norm_g"]), final_norm=True)

    shape_kv = (1, batch, WINDOW, N_KV, HEAD_DIM)
    return (y2.reshape(batch, seq_len, D_MODEL), conv_new,
            k_new.reshape(shape_kv), v_new.reshape(shape_kv)), late


def kernel(x_prompt, x_sample, c_prompt, c_sample, state_conv, cache_k, cache_v, norm_mix_g, norm_ffn_g,
           w_ada, b_ada, w_pw1, b_pw1, w_dw, b_dw, ln_g, ln_b, w_pw2, b_pw2, w_qkv, w_o, sinks,
           w_ffn_in, w_ffn_out, final_norm_g):
    bp, bs = c_prompt.shape[0], c_sample.shape[0]
    b_pad = -(-(bs + bp) // 8) * 8
    c_all = jnp.concatenate([c_sample, c_prompt, jnp.zeros((b_pad - bp - bs, D_MODEL), F32)], axis=0)
    mod = _adaln(c_all, w_ada, b_ada)
    mods = [mod[l].reshape(b_pad, 1, 6 * D_MODEL) for l in range(DEPTH)]

    heads = jnp.arange(1, N_HEADS + 1, dtype=F32)
    wts = dict(
        norm_mix_g=norm_mix_g, norm_ffn_g=norm_ffn_g, final_norm_g=final_norm_g,
        w_pw1=w_pw1[0].astype(BF16), b_pw1=b_pw1, w_dw=w_dw, b_dw=b_dw, ln_g=ln_g, ln_b=ln_b,
        w_pw2=w_pw2[0].astype(BF16), b_pw2=b_pw2, w_qkv=w_qkv, w_o=w_o, w_ffn_in=w_ffn_in, w_ffn_out=w_ffn_out,
        slopes=jnp.exp2(-8.0 * heads / N_HEADS), sinks=sinks[0].astype(F32),
    )
    (y_p, conv_p, k_p, v_p), late = _trunk(x_prompt, mods, bs, None, None, None, wts, None)
    (y_s, conv_s, k_s, v_s), _ = _trunk(x_sample, mods, 0, state_conv, cache_k, cache_v, wts, late)
    return (y_p, y_s, conv_p, conv_s, k_p, v_p, k_s, v_s)
```

```python
import functools

import jax
import jax.numpy as jnp
from jax import lax
from jax.experimental import pallas as pl
from jax.experimental.pallas import tpu as pltpu

F32 = jnp.float32
BF16 = jnp.bfloat16

D_MODEL = 2048
DEPTH = 2
CHUNK = 64
CONV_W = 31
N_HEADS = 32
N_KV = 4
HEAD_DIM = 64
GROUP = N_HEADS // N_KV
WINDOW = 128
D_FF = 5632
EPS = 1e-6
NEG = -1e30

VMEM_LIMIT_BYTES = 60 * 1024 * 1024
LANES = 128
HALO = 32
PAIR = 2 * HEAD_DIM
KPAD = 256

SHIFT1, SCALE1, GATE1, SHIFT2, SCALE2, GATE2 = range(6)


def _params(sem):
    return pltpu.CompilerParams(dimension_semantics=sem, vmem_limit_bytes=VMEM_LIMIT_BYTES)


def _silu(v):
    return v * jax.nn.sigmoid(v)


FILL_ROWS = 128


def _fill_rows(h_ref, x_ref, gain_ref, scale_ref, shift_ref, nb, start, n_rows):
    rows = x_ref.shape[0] // nb
    blk = min(rows, FILL_ROWS)
    for u in range(n_rows // blk):
        at = start + u * blk
        s = 0 if nb == 1 else at // rows
        rs = pl.ds(pl.multiple_of(at, blk), blk)
        x = x_ref[rs, :]
        y = x * lax.rsqrt(jnp.mean(x * x, axis=-1, keepdims=True) + EPS)
        mul = gain_ref[...] * (1.0 + scale_ref[s])
        h_ref[rs, :] = (y * mul + shift_ref[s]).astype(BF16)


def _fill_h(h_ref, x_ref, gain_ref, scale_ref, shift_ref, nb):
    def body(r, carry):
        _fill_rows(h_ref, x_ref, gain_ref, scale_ref, shift_ref, nb, r * FILL_ROWS, FILL_ROWS)
        return carry

    lax.fori_loop(0, x_ref.shape[0] // FILL_ROWS, body, 0)


def _by_parity(i, h_a, h_b, body):
    @pl.when(i % 2 == 0)
    def _():
        body(h_a, h_b)

    @pl.when(i % 2 == 1)
    def _():
        body(h_b, h_a)


def _adaln_kernel(c_ref, w_ref, b_ref, o_ref):
    c = c_ref[...]
    a = _silu(c).astype(BF16)
    o_ref[0] = jnp.dot(a, w_ref[0].astype(BF16), preferred_element_type=F32) + b_ref[0]


def _adaln(c_all, w_ada, b_ada, tn=1024):
    bp = c_all.shape[0]
    n = w_ada.shape[-1]
    return pl.pallas_call(
        _adaln_kernel,
        grid=(DEPTH, n // tn),
        in_specs=[
            pl.BlockSpec((bp, D_MODEL), lambda l, j: (0, 0)),
            pl.BlockSpec((1, D_MODEL, tn), lambda l, j: (l, 0, j)),
            pl.BlockSpec((1, 1, tn), lambda l, j: (l, 0, j)),
        ],
        out_specs=pl.BlockSpec((1, bp, tn), lambda l, j: (l, 0, j)),
        out_shape=jax.ShapeDtypeStruct((DEPTH, bp, n), F32),
        compiler_params=_params(("arbitrary", "arbitrary")),
        name="adaln",
    )(c_all, w_ada, b_ada.reshape(DEPTH, 1, n))


def _row_tiling(seq_len, tm):
    if tm >= seq_len:
        assert tm % seq_len == 0
        return tm // seq_len, 1
    assert seq_len % tm == 0
    return 1, seq_len // tm


def _mod_spec(which, row0, nb, tps, width=D_MODEL, tile=lambda i, j: i):
    assert row0 % nb == 0
    per = D_MODEL // width
    if per == 1:
        return pl.BlockSpec((nb, 1, width), lambda i, j: (row0 // nb + tile(i, j) // tps, 0, which))
    return pl.BlockSpec((nb, 1, width), lambda i, j: (row0 // nb + tile(i, j) // tps, 0, which * per + j))


def _prefetch_specs(tm, n_tiles, row0, nb, tps, scale, shift):
    first = lambda i, j: 0
    nxt = lambda i, j: jnp.minimum(i + 1, n_tiles - 1)
    return [
        pl.BlockSpec((tm, D_MODEL), lambda i, j: (0, 0), pipeline_mode=pl.Buffered(1)),
        pl.BlockSpec((tm, D_MODEL), lambda i, j: (nxt(i, j), 0)),
        _mod_spec(scale, row0, nb, tps, tile=first), _mod_spec(shift, row0, nb, tps, tile=first),
        _mod_spec(scale, row0, nb, tps, tile=nxt), _mod_spec(shift, row0, nb, tps, tile=nxt),
    ]


def _pipelined_h(h_a, h_b, x0_ref, xn_ref, gain_ref, scale0_ref, shift0_ref, scale_ref, shift_ref, nb, nj, use):
    i = pl.program_id(0)
    j = pl.program_id(1)

    @pl.when(jnp.logical_and(i == 0, j == 0))
    def _():
        _fill_h(h_a, x0_ref, gain_ref, scale0_ref, shift0_ref, nb)

    slab = xn_ref.shape[0] // nj

    def body(h_cur, h_next):
        _fill_rows(h_next, xn_ref, gain_ref, scale_ref, shift_ref, nb, j * slab, slab)
        use(h_cur[...])

    _by_parity(i, h_a, h_b, body)


def _pw1_glu_kernel(x0_ref, xn_ref, scale0_ref, shift0_ref, scale_ref, shift_ref, gain_ref,
                    wa_ref, wb_ref, ba_ref, bb_ref, g_ref, h_a, h_b, *, nb, nj):
    def use(h):
        a = jnp.dot(h, wa_ref[...], preferred_element_type=F32) + ba_ref[...]
        b = jnp.dot(h, wb_ref[...], preferred_element_type=F32) + bb_ref[...]
        g_ref[...] = a * jax.nn.sigmoid(b)

    _pipelined_h(h_a, h_b, x0_ref, xn_ref, gain_ref, scale0_ref, shift0_ref, scale_ref, shift_ref, nb, nj, use)


def _pw1_glu(x2, seq_len, gain, mod, row0, w1, b1, tm=1024, tn=512):
    rows = x2.shape[0]
    nb, tps = _row_tiling(seq_len, tm)
    nj = D_MODEL // tn
    return pl.pallas_call(
        functools.partial(_pw1_glu_kernel, nb=nb, nj=nj),
        grid=(rows // tm, nj),
        in_specs=_prefetch_specs(tm, rows // tm, row0, nb, tps, SCALE1, SHIFT1) + [
            pl.BlockSpec((1, D_MODEL), lambda i, j: (0, 0)),
            pl.BlockSpec((D_MODEL, tn), lambda i, j: (0, j)),
            pl.BlockSpec((D_MODEL, tn), lambda i, j: (0, nj + j)),
            pl.BlockSpec((1, tn), lambda i, j: (0, j)),
            pl.BlockSpec((1, tn), lambda i, j: (0, nj + j)),
        ],
        out_specs=pl.BlockSpec((tm, tn), lambda i, j: (i, j)),
        out_shape=jax.ShapeDtypeStruct((rows, D_MODEL), F32),
        scratch_shapes=[pltpu.VMEM((tm, D_MODEL), BF16), pltpu.VMEM((tm, D_MODEL), BF16)],
        compiler_params=_params(("arbitrary", "arbitrary")),
        name="pw1_glu",
    )(x2, x2, mod, mod, mod, mod, gain, w1, w1, b1, b1)


CONV_ROWS = 64
LN_ROWS = 128
NCHUNK = D_MODEL // LANES


def _conv_pw2_kernel(g_ref, halo_ref, prev_ref, x_ref, gate_ref, wdw_ref, bdw_ref, lng_ref, lnb_ref,
                     w2_ref, b2_ref, *rest, nb, tps, n_cast):
    cast_src = rest[:n_cast]
    o_ref = rest[n_cast]
    cast_dst = rest[n_cast + 1:2 * n_cast + 1]
    win_scr, y_scr, ybf_scr = rest[2 * n_cast + 1:]
    i = pl.program_id(0)
    j = pl.program_id(1)
    nj = pl.num_programs(1)
    tm = g_ref.shape[0]
    tt = tm // nb
    ti = jnp.minimum(i, pl.num_programs(0) - 2)
    cur = i % 2
    chunks_per_step = w2_ref.shape[1] // LANES

    for src, dst in zip(cast_src, cast_dst):
        dst[...] = src[...].astype(BF16)

    @pl.when(j == 0)
    def _():
        @pl.when(i == 0)
        def _():
            ybf_scr[1] = jnp.zeros(ybf_scr.shape[1:], BF16)

        def put_history(src_of):
            for s in range(nb):
                for c in range(NCHUNK):
                    win_scr[s * NCHUNK + c, 0:HALO, :] = src_of(s, c)

        if tps == 1:
            put_history(lambda s, c: prev_ref[s, :, c * LANES:(c + 1) * LANES])
        else:
            first = (ti % tps) == 0

            @pl.when(first)
            def _():
                put_history(lambda s, c: prev_ref[s, :, c * LANES:(c + 1) * LANES])

            @pl.when(jnp.logical_not(first))
            def _():
                put_history(lambda s, c: halo_ref[:, c * LANES:(c + 1) * LANES])
        for s in range(nb):
            for c in range(NCHUNK):
                win_scr[s * NCHUNK + c, HALO:HALO + tt, :] = g_ref[s * tt:(s + 1) * tt, c * LANES:(c + 1) * LANES]

    lead = HALO - (CONV_W - 1)
    for cc in range(chunks_per_step):
        c = j * chunks_per_step + cc
        for s in range(nb):
            for rb in range(tt // CONV_ROWS):
                acc = jnp.zeros((CONV_ROWS, LANES), F32) + bdw_ref[c]
                for k in range(CONV_W):
                    r0 = rb * CONV_ROWS + k + lead
                    acc = acc + win_scr[s * NCHUNK + c, r0:r0 + CONV_ROWS, :] * wdw_ref[c, k:k + 1, :]
                y_scr[c, s * tt + rb * CONV_ROWS:s * tt + (rb + 1) * CONV_ROWS, :] = acc

    out = jnp.dot(ybf_scr[1 - cur], w2_ref[...], preferred_element_type=F32) + b2_ref[...]
    for s in range(nb):
        sl = slice(s * tt, (s + 1) * tt)
        o_ref[sl, :] = x_ref[sl, :] + gate_ref[s] * out[sl, :]

    @pl.when(j == nj - 1)
    def _():
        def ln_block(r, carry):
            rs = pl.ds(pl.multiple_of(r * LN_ROWS, LN_ROWS), LN_ROWS)
            tot = y_scr[0, rs, :]
            for c in range(1, NCHUNK):
                tot = tot + y_scr[c, rs, :]
            mu = jnp.sum(tot, axis=-1, keepdims=True) * (1.0 / D_MODEL)
            sq = jnp.zeros((LN_ROWS, LANES), F32)
            for c in range(NCHUNK):
                yc = y_scr[c, rs, :] - mu
                sq = sq + yc * yc
            rstd = lax.rsqrt(jnp.sum(sq, axis=-1, keepdims=True) * (1.0 / D_MODEL) + EPS)
            for c in range(NCHUNK):
                cs = slice(c * LANES, (c + 1) * LANES)
                z = (y_scr[c, rs, :] - mu) * rstd * lng_ref[:, cs] + lnb_ref[:, cs]
                ybf_scr[cur, rs, cs] = _silu(z).astype(BF16)
            return carry

        lax.fori_loop(0, tm // LN_ROWS, ln_block, 0)


def _conv_pw2(g2, prev, x2, seq_len, mod, row0, w_dw, b_dw, ln_g, ln_b, w2, b2, casts=(), tm=512, tn=512):
    rows = g2.shape[0]
    nb, tps = _row_tiling(seq_len, tm)
    tt = tm // nb
    hb = tm // HALO
    nj = D_MODEL // tn
    n_tiles = rows // tm
    n_steps = n_tiles * nj
    cast_specs = []
    for a in casts:
        slab = a.shape[0] // n_steps
        assert a.shape[0] % n_steps == 0 and slab % 16 == 0, a.shape
        cast_specs.append(pl.BlockSpec((slab, a.shape[1]), lambda i, j: (jnp.minimum(i * nj + j, n_steps - 1), 0)))
    front = lambda i: jnp.minimum(i, n_tiles - 1)
    back = lambda i: jnp.maximum(i - 1, 0)
    assert row0 % nb == 0
    outs = pl.pallas_call(
        functools.partial(_conv_pw2_kernel, nb=nb, tps=tps, n_cast=len(casts)),
        grid=(n_tiles + 1, nj),
        in_specs=[
            pl.BlockSpec((tm, D_MODEL), lambda i, j: (front(i), 0)),
            pl.BlockSpec((HALO, D_MODEL), lambda i, j: (jnp.maximum(front(i) * hb - 1, 0), 0)),
            pl.BlockSpec((nb, HALO, D_MODEL), lambda i, j: (front(i) // tps, 0, 0)),
            pl.BlockSpec((tm, tn), lambda i, j: (back(i), j)),
            pl.BlockSpec((nb, 1, tn), lambda i, j: (row0 // nb + back(i) // tps, 0, GATE1 * nj + j)),
            pl.BlockSpec((NCHUNK, CONV_W, LANES), lambda i, j: (0, 0, 0)),
            pl.BlockSpec((NCHUNK, 1, LANES), lambda i, j: (0, 0, 0)),
            pl.BlockSpec((1, D_MODEL), lambda i, j: (0, 0)),
            pl.BlockSpec((1, D_MODEL), lambda i, j: (0, 0)),
            pl.BlockSpec((D_MODEL, tn), lambda i, j: (0, j)),
            pl.BlockSpec((1, tn), lambda i, j: (0, j)),
        ] + cast_specs,
        out_specs=[pl.BlockSpec((tm, tn), lambda i, j: (back(i), j * jnp.minimum(i, 1)))] + cast_specs,
        out_shape=[jax.ShapeDtypeStruct((rows, D_MODEL), F32)]
        + [jax.ShapeDtypeStruct(a.shape, BF16) for a in casts],
        scratch_shapes=[
            pltpu.VMEM((nb * NCHUNK, HALO + tt, LANES), F32),
            pltpu.VMEM((NCHUNK, tm, LANES), F32),
            pltpu.VMEM((2, tm, D_MODEL), BF16),
        ],
        compiler_params=_params(("arbitrary", "arbitrary")),
        name="conv_pw2",
    )(g2, g2, prev, x2, mod,
      w_dw.reshape(CONV_W, NCHUNK, LANES).transpose(1, 0, 2), b_dw.reshape(NCHUNK, 1, LANES),
      ln_g, ln_b, w2, b2, *casts)
    return outs[0], list(outs[1:])


FFN_SUB = 512


def _ffn_kernel(x_ref, gain_ref, scale_ref, shift_ref, gate_ref, wa_ref, wb_ref, wo_ref, fin_ref,
                o_ref, h_a, h_b, *, nb, final_norm):
    i = pl.program_id(0)
    f = pl.program_id(1)
    tm = x_ref.shape[0]
    rows = tm // nb

    @pl.when(f == 0)
    def _():
        @pl.when(i == 0)
        def _():
            _fill_h(h_a, x_ref, gain_ref, scale_ref, shift_ref, nb)

        o_ref[...] = x_ref[...]

    slab = jnp.clip(f - 1, 0, tm // FILL_ROWS - 1)

    def body(h_cur, h_next):
        _fill_rows(h_next, x_ref, gain_ref, scale_ref, shift_ref, nb, slab * FILL_ROWS, FILL_ROWS)
        for m in range(tm // FFN_SUB):
            h = h_cur[m * FFN_SUB:(m + 1) * FFN_SUB, :]
            a = jnp.dot(h, wa_ref[...], preferred_element_type=F32)
            b = jnp.dot(h, wb_ref[...], preferred_element_type=F32)
            act = (_silu(a) * b).astype(BF16)
            res = jnp.dot(act, wo_ref[...], preferred_element_type=F32)
            piece = min(rows, FFN_SUB)
            for u in range(FFN_SUB // piece):
                lo = m * FFN_SUB + u * piece
                o_ref[lo:lo + piece, :] += gate_ref[lo // rows] * res[u * piece:(u + 1) * piece, :]

    _by_parity(i, h_a, h_b, body)

    if final_norm:
        @pl.when(f == pl.num_programs(1) - 1)
        def _():
            def body(r, carry):
                rs = pl.ds(pl.multiple_of(r * FILL_ROWS, FILL_ROWS), FILL_ROWS)
                y = o_ref[rs, :]
                o_ref[rs, :] = y * lax.rsqrt(jnp.mean(y * y, axis=-1, keepdims=True) + EPS) * fin_ref[...]
                return carry

            lax.fori_loop(0, tm // FILL_ROWS, body, 0)


def _ffn(x2, seq_len, gain, mod, row0, w_in, w_out, layer, fin_gain, final_norm, tm=1024, tf=512):
    rows = x2.shape[0]
    nb, tps = _row_tiling(seq_len, tm)
    nf = D_FF // tf
    n_tiles = rows // tm
    assert nf - 1 >= tm // FILL_ROWS
    vec_spec = pl.BlockSpec((1, D_MODEL), lambda i, f: (0, 0))
    x_tile = lambda i, f: jnp.minimum(i + jnp.minimum(f, 1), n_tiles - 1)
    return pl.pallas_call(
        functools.partial(_ffn_kernel, nb=nb, final_norm=final_norm),
        grid=(n_tiles, nf),
        in_specs=[
            pl.BlockSpec((tm, D_MODEL), lambda i, f: (x_tile(i, f), 0), pipeline_mode=pl.Buffered(1)),
            vec_spec,
            _mod_spec(SCALE2, row0, nb, tps, tile=x_tile), _mod_spec(SHIFT2, row0, nb, tps, tile=x_tile),
            _mod_spec(GATE2, row0, nb, tps),
            pl.BlockSpec((None, D_MODEL, tf), lambda i, f: (layer, 0, f)),
            pl.BlockSpec((None, D_MODEL, tf), lambda i, f: (layer, 0, nf + f)),
            pl.BlockSpec((None, tf, D_MODEL), lambda i, f: (layer, f, 0)),
            vec_spec,
        ],
        out_specs=pl.BlockSpec((tm, D_MODEL), lambda i, f: (i, 0)),
        out_shape=jax.ShapeDtypeStruct((rows, D_MODEL), F32),
        scratch_shapes=[pltpu.VMEM((tm, D_MODEL), BF16), pltpu.VMEM((tm, D_MODEL), BF16)],
        compiler_params=_params(("arbitrary", "arbitrary")),
        name="ffn",
    )(x2, gain, mod, mod, mod, w_in, w_in, w_out, fin_gain)


def _qkv_kernel(x0_ref, xn_ref, scale0_ref, shift0_ref, scale_ref, shift_ref, gain_ref,
                wq_ref, wkv_ref, q_ref, kv_ref, h_a, h_b, *, nb, nj):
    def use(h):
        @pl.when(pl.program_id(1) == 0)
        def _():
            kv_ref[...] = jnp.dot(h, wkv_ref[...], preferred_element_type=F32)

        q = jnp.dot(h, wq_ref[...], preferred_element_type=F32)
        q_ref[...] = (q * (HEAD_DIM ** -0.5)).astype(BF16)

    _pipelined_h(h_a, h_b, x0_ref, xn_ref, gain_ref, scale0_ref, shift0_ref, scale_ref, shift_ref, nb, nj, use)


def _qkv(x2, seq_len, gain, mod, row0, w_qkv, tm=1024, tn=512):
    rows = x2.shape[0]
    nb, tps = _row_tiling(seq_len, tm)
    nq = (N_HEADS * HEAD_DIM) // tn
    kv_cols = 2 * N_KV * HEAD_DIM
    assert kv_cols == tn
    return pl.pallas_call(
        functools.partial(_qkv_kernel, nb=nb, nj=nq),
        grid=(rows // tm, nq),
        in_specs=_prefetch_specs(tm, rows // tm, row0, nb, tps, SCALE1, SHIFT1) + [
            pl.BlockSpec((1, D_MODEL), lambda i, j: (0, 0)),
            pl.BlockSpec((D_MODEL, tn), lambda i, j: (0, j)),
            pl.BlockSpec((D_MODEL, kv_cols), lambda i, j: (0, nq)),
        ],
        out_specs=[
            pl.BlockSpec((tm, tn), lambda i, j: (i, j)),
            pl.BlockSpec((tm, kv_cols), lambda i, j: (i, 0)),
        ],
        out_shape=[
            jax.ShapeDtypeStruct((rows, N_HEADS * HEAD_DIM), BF16),
            jax.ShapeDtypeStruct((rows, kv_cols), F32),
        ],
        scratch_shapes=[pltpu.VMEM((tm, D_MODEL), BF16), pltpu.VMEM((tm, D_MODEL), BF16)],
        compiler_params=_params(("arbitrary", "arbitrary")),
        name="qkv",
    )(x2, x2, mod, mod, mod, mod, gain, w_qkv, w_qkv)


PAIRS_PER_KV = GROUP // 2
QROWS = PAIRS_PER_KV * CHUNK
BAND = WINDOW + CHUNK
ATT_G = 8
KVW = N_KV * HEAD_DIM


def _attn_kernel(slope_ref, sink_ref, q_ref, *refs, band, steps_per_seq):
    if band:
        kp2_ref, kp1_ref, kc_ref, vp2_ref, vp1_ref, vc_ref = refs[:6]
        refs = refs[6:]
    else:
        ck_ref, kn_ref, cv_ref, vn_ref = refs[:4]
        refs = refs[4:]
    o_ref, bias_scr, ones_scr, kl_scr, kr_scr, vl_scr, vr_scr = refs
    step = pl.program_id(0)

    @pl.when(step == 0)
    def _():
        row = lax.broadcasted_iota(jnp.int32, (QROWS, 2 * KPAD), 0)
        col = lax.broadcasted_iota(jnp.int32, (QROWS, 2 * KPAD), 1)
        key = jnp.bitwise_and(col, KPAD - 1)
        dist = jnp.abs(WINDOW + jnp.bitwise_and(row, CHUNK - 1) - key).astype(F32)
        for kv in range(N_KV):
            slope = jnp.zeros((QROWS, 2 * KPAD), F32)
            sink = jnp.zeros((QROWS, 2 * KPAD), F32)
            for pr in range(PAIRS_PER_KV):
                in_pair = jnp.logical_and(row >= pr * CHUNK, row < (pr + 1) * CHUNK)
                in_a = jnp.logical_and(in_pair, col < KPAD)
                in_b = jnp.logical_and(in_pair, col >= KPAD)
                head = kv * GROUP + pr * 2
                slope = jnp.where(in_a, slope_ref[head], jnp.where(in_b, slope_ref[head + 1], slope))
                sink = jnp.where(in_a, sink_ref[head], jnp.where(in_b, sink_ref[head + 1], sink))
            bias_scr[kv] = jnp.where(key < BAND, -(slope * dist), jnp.where(key == BAND, sink, NEG))
        orow = lax.broadcasted_iota(jnp.int32, (2 * KPAD, PAIR), 0)
        ocol = lax.broadcasted_iota(jnp.int32, (2 * KPAD, PAIR), 1)
        in_first = ocol < HEAD_DIM
        ones_scr[...] = jnp.where(orow < KPAD, jnp.where(in_first, 1.0, 0.0),
                                  jnp.where(in_first, 0.0, 1.0)).astype(BF16)

    lane64 = lax.broadcasted_iota(jnp.int32, (CHUNK, PAIR), 1) < HEAD_DIM

    def prep(src_ref, r_src, col0, r_dst, l_scr, r_scr):
        for lb in range(N_KV // 2):
            blk = src_ref[r_src:r_src + CHUNK, col0 + lb * PAIR:col0 + (lb + 1) * PAIR]
            ev = jnp.where(lane64, blk, 0.0)
            od = jnp.where(lane64, 0.0, blk)
            ds = slice(r_dst, r_dst + CHUNK)
            l_scr[2 * lb, ds, :] = ev.astype(BF16)
            r_scr[2 * lb, ds, :] = pltpu.roll(ev, HEAD_DIM, axis=1).astype(BF16)
            r_scr[2 * lb + 1, ds, :] = od.astype(BF16)
            l_scr[2 * lb + 1, ds, :] = pltpu.roll(od, HEAD_DIM, axis=1).astype(BF16)

    if band:
        prep(kp2_ref, 0, 0, 0, kl_scr, kr_scr)
        prep(kp1_ref, 0, 0, CHUNK, kl_scr, kr_scr)
        prep(vp2_ref, 0, 0, 0, vl_scr, vr_scr)
        prep(vp1_ref, 0, 0, CHUNK, vl_scr, vr_scr)
        for g in range(ATT_G):
            prep(kc_ref, g * CHUNK, 0, WINDOW + g * CHUNK, kl_scr, kr_scr)
            prep(vc_ref, g * CHUNK, 0, WINDOW + g * CHUNK, vl_scr, vr_scr)
    else:
        for g in range(ATT_G):
            for h in range(WINDOW // CHUNK):
                prep(ck_ref, g * WINDOW + h * CHUNK, 0, g * BAND + h * CHUNK, kl_scr, kr_scr)
                prep(cv_ref, g * WINDOW + h * CHUNK, 0, g * BAND + h * CHUNK, vl_scr, vr_scr)
            prep(kn_ref, g * CHUNK, 0, g * BAND + WINDOW, kl_scr, kr_scr)
            prep(vn_ref, g * CHUNK, 0, g * BAND + WINDOW, vl_scr, vr_scr)

    zpad = jnp.zeros((KPAD - BAND, PAIR), BF16)
    keyrow = jnp.bitwise_and(lax.broadcasted_iota(jnp.int32, (1, 2 * KPAD), 1), KPAD - 1)
    n0 = (step % steps_per_seq) * ATT_G

    def chunk(c, carry, masked):
        r0 = pl.multiple_of(c * CHUNK, CHUNK)
        k0 = r0 if band else pl.multiple_of(c * BAND, CHUNK)
        ks = pl.ds(k0, BAND)
        qrows = pl.ds(r0, CHUNK)
        if masked:
            maskrow = jnp.where(keyrow >= WINDOW - (n0 + c) * CHUNK, 0.0, NEG)
        for kv in range(N_KV):
            qs = jnp.concatenate(
                [q_ref[qrows, kv * GROUP * HEAD_DIM + pr * PAIR: kv * GROUP * HEAD_DIM + (pr + 1) * PAIR]
                 for pr in range(PAIRS_PER_KV)], axis=0)
            kbd = jnp.concatenate([kl_scr[kv, ks, :], zpad, kr_scr[kv, ks, :], zpad], axis=0)
            vbd = jnp.concatenate([vl_scr[kv, ks, :], zpad, vr_scr[kv, ks, :], zpad], axis=0)
            s = lax.dot_general(qs, kbd, (((1,), (1,)), ((), ())), preferred_element_type=F32)
            s = s + bias_scr[kv]
            if masked:
                s = s + maskrow
            s_a = s[:, :KPAD]
            s_b = s[:, KPAD:]
            m_a = jnp.max(s_a, axis=-1, keepdims=True)
            m_b = jnp.max(s_b, axis=-1, keepdims=True)
            p = jnp.concatenate([jnp.exp(s_a - m_a), jnp.exp(s_b - m_b)], axis=1).astype(BF16)
            ol = jnp.dot(p, jnp.concatenate([vbd, ones_scr[...]], axis=1), preferred_element_type=F32)
            o = ol[:, :PAIR] / ol[:, PAIR:]
            for pr in range(PAIRS_PER_KV):
                c0 = kv * GROUP * HEAD_DIM + pr * PAIR
                o_ref[qrows, c0:c0 + PAIR] = o[pr * CHUNK:(pr + 1) * CHUNK, :].astype(BF16)
        return carry

    n_lead = WINDOW // CHUNK if band else 0
    if n_lead:
        lax.fori_loop(0, n_lead, functools.partial(chunk, masked=True), 0, unroll=2)
    lax.fori_loop(n_lead, ATT_G, functools.partial(chunk, masked=False), 0, unroll=2)


def _attention(q, kv, cache_k, cache_v, slopes, sinks, seq_len):
    rows = q.shape[0]
    tq = ATT_G * CHUNK
    band = cache_k is None
    smem = pl.BlockSpec(memory_space=pltpu.SMEM)
    if band:
        assert seq_len % tq == 0
        sps = seq_len // tq
        cps = seq_len // CHUNK

        def prev(d, col):
            return pl.BlockSpec((CHUNK, KVW), lambda i: (jnp.maximum((i % sps) * ATT_G - d, 0) + (i // sps) * cps, col))

        kv_specs = [prev(2, 0), prev(1, 0), pl.BlockSpec((tq, KVW), lambda i: (i, 0)),
                    prev(2, 1), prev(1, 1), pl.BlockSpec((tq, KVW), lambda i: (i, 1))]
        kv_args = [kv] * 6
        key_rows = WINDOW + tq
    else:
        assert seq_len == CHUNK
        sps = 1
        kv_specs = [pl.BlockSpec((ATT_G * WINDOW, KVW), lambda i: (i, 0)), pl.BlockSpec((tq, KVW), lambda i: (i, 0)),
                    pl.BlockSpec((ATT_G * WINDOW, KVW), lambda i: (i, 0)), pl.BlockSpec((tq, KVW), lambda i: (i, 1))]
        kv_args = [cache_k, kv, cache_v, kv]
        key_rows = ATT_G * BAND
    side = pltpu.VMEM((N_KV, key_rows, PAIR), BF16)
    return pl.pallas_call(
        functools.partial(_attn_kernel, band=band, steps_per_seq=sps),
        grid=(rows // tq,),
        in_specs=[smem, smem, pl.BlockSpec((tq, N_HEADS * HEAD_DIM), lambda i: (i, 0))] + kv_specs,
        out_specs=pl.BlockSpec((tq, N_HEADS * HEAD_DIM), lambda i: (i, 0)),
        out_shape=jax.ShapeDtypeStruct((rows, N_HEADS * HEAD_DIM), BF16),
        scratch_shapes=[
            pltpu.VMEM((N_KV, QROWS, 2 * KPAD), F32),
            pltpu.VMEM((2 * KPAD, PAIR), BF16),
            side, side, side, side,
        ],
        compiler_params=_params(("arbitrary",)),
        name="attention",
    )(slopes, sinks, q, *kv_args)


def _wo_kernel(a_ref, w_ref, x_ref, gate_ref, o_ref, *, nb):
    out = jnp.dot(a_ref[...], w_ref[...], preferred_element_type=F32)
    rows = a_ref.shape[0] // nb
    for s in range(nb):
        sl = slice(s * rows, (s + 1) * rows)
        o_ref[sl, :] = x_ref[sl, :] + gate_ref[s] * out[sl, :]


def _wo(attn, x2, seq_len, mod, row0, w_o, tm=1024, tn=1024):
    rows = x2.shape[0]
    nb, tps = _row_tiling(seq_len, tm)
    return pl.pallas_call(
        functools.partial(_wo_kernel, nb=nb),
        grid=(rows // tm, D_MODEL // tn),
        in_specs=[
            pl.BlockSpec((tm, N_HEADS * HEAD_DIM), lambda i, j: (i, 0)),
            pl.BlockSpec((N_HEADS * HEAD_DIM, tn), lambda i, j: (0, j)),
            pl.BlockSpec((tm, tn), lambda i, j: (i, j)),
            _mod_spec(GATE1, row0, nb, tps, width=tn),
        ],
        out_specs=pl.BlockSpec((tm, tn), lambda i, j: (i, j)),
        out_shape=jax.ShapeDtypeStruct((rows, D_MODEL), F32),
        compiler_params=_params(("arbitrary", "arbitrary")),
        name="wo",
    )(attn, w_o, x2, mod)


def _trunk(x, mods, row0, state_conv, cache_k, cache_v, wts, late):
    batch, seq_len, _ = x.shape
    rows = batch * seq_len
    x2 = x.reshape(rows, D_MODEL)
    prompt = state_conv is None
    row = lambda v: v.reshape(1, -1)
    assert seq_len >= CONV_W - 1 and seq_len >= WINDOW // 2

    g = _pw1_glu(x2, seq_len, row(wts["norm_mix_g"][0]), mods[0], row0, wts["w_pw1"], row(wts["b_pw1"][0]))
    if prompt:
        prev = jnp.zeros((batch, HALO, D_MODEL), F32)
    else:
        prev = jnp.pad(state_conv[0], ((0, 0), (HALO - (CONV_W - 1), 0), (0, 0)))
    conv_new = g.reshape(batch, seq_len, D_MODEL)[:, -(CONV_W - 1):][None]
    casts = ()
    if late is None:
        casts = (wts["w_ffn_in"].reshape(DEPTH * D_MODEL, 2 * D_FF), wts["w_ffn_out"].reshape(DEPTH * D_FF, D_MODEL),
                 wts["w_qkv"][0], wts["w_o"][0])
    x2, cast_out = _conv_pw2(g, prev, x2, seq_len, mods[0], row0, wts["w_dw"][0], row(wts["b_dw"][0]),
                             row(wts["ln_g"][0]), row(wts["ln_b"][0]), wts["w_pw2"], row(wts["b_pw2"][0]), casts)
    if late is None:
        late = dict(w_ffn_in=cast_out[0].reshape(DEPTH, D_MODEL, 2 * D_FF),
                    w_ffn_out=cast_out[1].reshape(DEPTH, D_FF, D_MODEL), w_qkv=cast_out[2], w_o=cast_out[3])
    x2 = _ffn(x2, seq_len, row(wts["norm_ffn_g"][0]), mods[0], row0, late["w_ffn_in"], late["w_ffn_out"], 0,
              row(wts["final_norm_g"]), final_norm=False)

    q, kv = _qkv(x2, seq_len, row(wts["norm_mix_g"][1]), mods[1], row0, late["w_qkv"])
    kv3 = kv.reshape(batch, seq_len, 2 * KVW)
    if prompt:
        attn = _attention(q, kv, None, None, wts["slopes"], wts["sinks"], seq_len)
        k_new = kv3[:, -WINDOW:, :KVW]
        v_new = kv3[:, -WINDOW:, KVW:]
    else:
        ck = cache_k[0].reshape(batch * WINDOW, KVW)
        cv = cache_v[0].reshape(batch * WINDOW, KVW)
        attn = _attention(q, kv, ck, cv, wts["slopes"], wts["sinks"], seq_len)
        keep = WINDOW - seq_len
        k_new = jnp.concatenate([cache_k[0].reshape(batch, WINDOW, KVW)[:, -keep:], kv3[:, :, :KVW]], axis=1)
        v_new = jnp.concatenate([cache_v[0].reshape(batch, WINDOW, KVW)[:, -keep:], kv3[:, :, KVW:]], axis=1)
    x2 = _wo(attn, x2, seq_len, mods[1], row0, late["w_o"])
    y2 = _ffn(x2, seq_len, row(wts["norm_ffn_g"][1]), mods[1], row0, late["w_ffn_in"], late["w_ffn_out"], 1,
              row(wts["final_norm_g"]), final_norm=True)

    shape_kv = (1, batch, WINDOW, N_KV, HEAD_DIM)
    return (y2.reshape(batch, seq_len, D_MODEL), conv_new,
            k_new.reshape(shape_kv), v_new.reshape(shape_kv)), late


def kernel(x_prompt, x_sample, c_prompt, c_sample, state_conv, cache_k, cache_v, norm_mix_g, norm_ffn_g,
           w_ada, b_ada, w_pw1, b_pw1, w_dw, b_dw, ln_g, ln_b, w_pw2, b_pw2, w_qkv, w_o, sinks,
           w_ffn_in, w_ffn_out, final_norm_g):
    bp, bs = c_prompt.shape[0], c_sample.shape[0]
    b_pad = -(-(bs + bp) // 8) * 8
    c_all = jnp.concatenate([c_sample, c_prompt, jnp.zeros((b_pad - bp - bs, D_MODEL), F32)], axis=0)
    mod = _adaln(c_all, w_ada, b_ada)
    mods = [mod[l].reshape(b_pad, 1, 6 * D_MODEL) for l in range(DEPTH)]

    heads = jnp.arange(1, N_HEADS + 1, dtype=F32)
    wts = dict(
        norm_mix_g=norm_mix_g, norm_ffn_g=norm_ffn_g, final_norm_g=final_norm_g,
        w_pw1=w_pw1[0].astype(BF16), b_pw1=b_pw1, w_dw=w_dw, b_dw=b_dw, ln_g=ln_g, ln_b=ln_b,
        w_pw2=w_pw2[0].astype(BF16), b_pw2=b_pw2, w_qkv=w_qkv, w_o=w_o, w_ffn_in=w_ffn_in, w_ffn_out=w_ffn_out,
        slopes=jnp.exp2(-8.0 * heads / N_HEADS), sinks=sinks[0].astype(F32),
    )
    (y_p, conv_p, k_p, v_p), late = _trunk(x_prompt, mods, bs, None, None, None, wts, None)
    (y_s, conv_s, k_s, v_s), _ = _trunk(x_sample, mods, 0, state_conv, cache_k, cache_v, wts, late)
    return (y_p, y_s, conv_p, conv_s, k_p, v_p, k_s, v_s)
```

```python
import functools

import jax
import jax.numpy as jnp
from jax import lax
from jax.experimental import pallas as pl
from jax.experimental.pallas import tpu as pltpu

F32 = jnp.float32
BF16 = jnp.bfloat16

D_MODEL = 2048
DEPTH = 2
CHUNK = 64
CONV_W = 31
N_HEADS = 32
N_KV = 4
HEAD_DIM = 64
GROUP = N_HEADS // N_KV
WINDOW = 128
D_FF = 5632
EPS = 1e-6
NEG = -1e30

VMEM_LIMIT_BYTES = 60 * 1024 * 1024
LANES = 128
HALO = 32
PAIR = 2 * HEAD_DIM
KPAD = 256

SHIFT1, SCALE1, GATE1, SHIFT2, SCALE2, GATE2 = range(6)


def _params(sem):
    return pltpu.CompilerParams(dimension_semantics=sem, vmem_limit_bytes=VMEM_LIMIT_BYTES)


def _silu(v):
    return v * jax.nn.sigmoid(v)


FILL_ROWS = 128


def _fill_rows(h_ref, x_ref, gain_ref, scale_ref, shift_ref, nb, start, n_rows):
    rows = x_ref.shape[0] // nb
    blk = min(rows, FILL_ROWS)
    for u in range(n_rows // blk):
        at = start + u * blk
        s = 0 if nb == 1 else at // rows
        rs = pl.ds(pl.multiple_of(at, blk), blk)
        x = x_ref[rs, :]
        y = x * lax.rsqrt(jnp.mean(x * x, axis=-1, keepdims=True) + EPS)
        mul = gain_ref[...] * (1.0 + scale_ref[s])
        h_ref[rs, :] = (y * mul + shift_ref[s]).astype(BF16)


def _fill_h(h_ref, x_ref, gain_ref, scale_ref, shift_ref, nb):
    def body(r, carry):
        _fill_rows(h_ref, x_ref, gain_ref, scale_ref, shift_ref, nb, r * FILL_ROWS, FILL_ROWS)
        return carry

    lax.fori_loop(0, x_ref.shape[0] // FILL_ROWS, body, 0)


def _by_parity(i, h_a, h_b, body):
    @pl.when(i % 2 == 0)
    def _():
        body(h_a, h_b)

    @pl.when(i % 2 == 1)
    def _():
        body(h_b, h_a)


def _adaln_kernel(c_ref, w_ref, b_ref, o_ref):
    c = c_ref[...]
    a = _silu(c).astype(BF16)
    o_ref[0] = jnp.dot(a, w_ref[0].astype(BF16), preferred_element_type=F32) + b_ref[0]


def _adaln(c_all, w_ada, b_ada, tn=1024):
    bp = c_all.shape[0]
    n = w_ada.shape[-1]
    return pl.pallas_call(
        _adaln_kernel,
        grid=(DEPTH, n // tn),
        in_specs=[
            pl.BlockSpec((bp, D_MODEL), lambda l, j: (0, 0)),
            pl.BlockSpec((1, D_MODEL, tn), lambda l, j: (l, 0, j)),
            pl.BlockSpec((1, 1, tn), lambda l, j: (l, 0, j)),
        ],
        out_specs=pl.BlockSpec((1, bp, tn), lambda l, j: (l, 0, j)),
        out_shape=jax.ShapeDtypeStruct((DEPTH, bp, n), F32),
        compiler_params=_params(("arbitrary", "arbitrary")),
        name="adaln",
    )(c_all, w_ada, b_ada.reshape(DEPTH, 1, n))


def _row_tiling(seq_len, tm):
    if tm >= seq_len:
        assert tm % seq_len == 0
        return tm // seq_len, 1
    assert seq_len % tm == 0
    return 1, seq_len // tm


def _mod_spec(which, row0, nb, tps, width=D_MODEL, tile=lambda i, j: i):
    assert row0 % nb == 0
    per = D_MODEL // width
    if per == 1:
        return pl.BlockSpec((nb, 1, width), lambda i, j: (row0 // nb + tile(i, j) // tps, 0, which))
    return pl.BlockSpec((nb, 1, width), lambda i, j: (row0 // nb + tile(i, j) // tps, 0, which * per + j))


def _pw1_glu_kernel(x_ref, gain_ref, scale_ref, shift_ref, wa_ref, wb_ref, ba_ref, bb_ref,
                    g_ref, h_scr, *, nb):
    @pl.when(pl.program_id(1) == 0)
    def _():
        _fill_h(h_scr, x_ref, gain_ref, scale_ref, shift_ref, nb)

    h = h_scr[...]
    a = jnp.dot(h, wa_ref[...], preferred_element_type=F32) + ba_ref[...]
    b = jnp.dot(h, wb_ref[...], preferred_element_type=F32) + bb_ref[...]
    g_ref[...] = a * jax.nn.sigmoid(b)


def _pw1_glu(x2, seq_len, gain, mod, row0, w1, b1, tm=1024, tn=512):
    rows = x2.shape[0]
    nb, tps = _row_tiling(seq_len, tm)
    nj = D_MODEL // tn
    return pl.pallas_call(
        functools.partial(_pw1_glu_kernel, nb=nb),
        grid=(rows // tm, nj),
        in_specs=[
            pl.BlockSpec((tm, D_MODEL), lambda i, j: (i, 0)),
            pl.BlockSpec((1, D_MODEL), lambda i, j: (0, 0)),
            _mod_spec(SCALE1, row0, nb, tps), _mod_spec(SHIFT1, row0, nb, tps),
            pl.BlockSpec((D_MODEL, tn), lambda i, j: (0, j)),
            pl.BlockSpec((D_MODEL, tn), lambda i, j: (0, nj + j)),
            pl.BlockSpec((1, tn), lambda i, j: (0, j)),
            pl.BlockSpec((1, tn), lambda i, j: (0, nj + j)),
        ],
        out_specs=pl.BlockSpec((tm, tn), lambda i, j: (i, j)),
        out_shape=jax.ShapeDtypeStruct((rows, D_MODEL), F32),
        scratch_shapes=[pltpu.VMEM((tm, D_MODEL), BF16)],
        compiler_params=_params(("arbitrary", "arbitrary")),
        name="pw1_glu",
    )(x2, gain, mod, mod, w1, w1, b1, b1)


CONV_ROWS = 64
LN_ROWS = 128
NCHUNK = D_MODEL // LANES


def _conv_pw2_kernel(g_ref, halo_ref, prev_ref, x_ref, gate_ref, wdw_ref, bdw_ref, lng_ref, lnb_ref,
                     w2_ref, b2_ref, *rest, nb, tps, n_cast):
    cast_src = rest[:n_cast]
    o_ref = rest[n_cast]
    cast_dst = rest[n_cast + 1:2 * n_cast + 1]
    win_scr, y_scr, ybf_scr = rest[2 * n_cast + 1:]
    i = pl.program_id(0)
    j = pl.program_id(1)
    nj = pl.num_programs(1)
    tm = g_ref.shape[0]
    tt = tm // nb
    ti = jnp.minimum(i, pl.num_programs(0) - 2)
    cur = i % 2
    chunks_per_step = w2_ref.shape[1] // LANES

    for src, dst in zip(cast_src, cast_dst):
        dst[...] = src[...].astype(BF16)

    @pl.when(j == 0)
    def _():
        @pl.when(i == 0)
        def _():
            ybf_scr[1] = jnp.zeros(ybf_scr.shape[1:], BF16)

        def put_history(src_of):
            for s in range(nb):
                for c in range(NCHUNK):
                    win_scr[s * NCHUNK + c, 0:HALO, :] = src_of(s, c)

        if tps == 1:
            put_history(lambda s, c: prev_ref[s, :, c * LANES:(c + 1) * LANES])
        else:
            first = (ti % tps) == 0

            @pl.when(first)
            def _():
                put_history(lambda s, c: prev_ref[s, :, c * LANES:(c + 1) * LANES])

            @pl.when(jnp.logical_not(first))
            def _():
                put_history(lambda s, c: halo_ref[:, c * LANES:(c + 1) * LANES])
        for s in range(nb):
            for c in range(NCHUNK):
                win_scr[s * NCHUNK + c, HALO:HALO + tt, :] = g_ref[s * tt:(s + 1) * tt, c * LANES:(c + 1) * LANES]

    lead = HALO - (CONV_W - 1)

    def conv_and_matmul(jj):
        for cc in range(chunks_per_step):
            c = jj * chunks_per_step + cc
            for s in range(nb):
                for rb in range(tt // CONV_ROWS):
                    acc = jnp.zeros((CONV_ROWS, LANES), F32) + bdw_ref[c]
                    for k in range(CONV_W):
                        r0 = rb * CONV_ROWS + k + lead
                        acc = acc + win_scr[s * NCHUNK + c, r0:r0 + CONV_ROWS, :] * wdw_ref[c, k:k + 1, :]
                    y_scr[c, s * tt + rb * CONV_ROWS:s * tt + (rb + 1) * CONV_ROWS, :] = acc

        out = jnp.dot(ybf_scr[1 - cur], w2_ref[...], preferred_element_type=F32) + b2_ref[...]
        for s in range(nb):
            sl = slice(s * tt, (s + 1) * tt)
            o_ref[sl, :] = x_ref[sl, :] + gate_ref[s] * out[sl, :]

    for jj in range(NCHUNK // chunks_per_step):
        pl.when(j == jj)(functools.partial(conv_and_matmul, jj))

    @pl.when(j == nj - 1)
    def _():
        def ln_block(r, carry):
            rs = pl.ds(pl.multiple_of(r * LN_ROWS, LN_ROWS), LN_ROWS)
            tot = y_scr[0, rs, :]
            for c in range(1, NCHUNK):
                tot = tot + y_scr[c, rs, :]
            mu = jnp.sum(tot, axis=-1, keepdims=True) * (1.0 / D_MODEL)
            sq = jnp.zeros((LN_ROWS, LANES), F32)
            for c in range(NCHUNK):
                yc = y_scr[c, rs, :] - mu
                sq = sq + yc * yc
            rstd = lax.rsqrt(jnp.sum(sq, axis=-1, keepdims=True) * (1.0 / D_MODEL) + EPS)
            for c in range(NCHUNK):
                cs = slice(c * LANES, (c + 1) * LANES)
                z = (y_scr[c, rs, :] - mu) * rstd * lng_ref[:, cs] + lnb_ref[:, cs]
                ybf_scr[cur, rs, cs] = _silu(z).astype(BF16)
            return carry

        lax.fori_loop(0, tm // LN_ROWS, ln_block, 0)


def _conv_pw2(g2, prev, x2, seq_len, mod, row0, w_dw, b_dw, ln_g, ln_b, w2, b2, casts=(), tm=512, tn=512):
    rows = g2.shape[0]
    nb, tps = _row_tiling(seq_len, tm)
    tt = tm // nb
    hb = tm // HALO
    nj = D_MODEL // tn
    n_tiles = rows // tm
    n_steps = n_tiles * nj
    cast_specs = []
    for a in casts:
        slab = a.shape[0] // n_steps
        assert a.shape[0] % n_steps == 0 and slab % 16 == 0, a.shape
        cast_specs.append(pl.BlockSpec((slab, a.shape[1]), lambda i, j: (jnp.minimum(i * nj + j, n_steps - 1), 0)))
    front = lambda i: jnp.minimum(i, n_tiles - 1)
    back = lambda i: jnp.maximum(i - 1, 0)
    assert row0 % nb == 0
    outs = pl.pallas_call(
        functools.partial(_conv_pw2_kernel, nb=nb, tps=tps, n_cast=len(casts)),
        grid=(n_tiles + 1, nj),
        in_specs=[
            pl.BlockSpec((tm, D_MODEL), lambda i, j: (front(i), 0)),
            pl.BlockSpec((HALO, D_MODEL), lambda i, j: (jnp.maximum(front(i) * hb - 1, 0), 0)),
            pl.BlockSpec((nb, HALO, D_MODEL), lambda i, j: (front(i) // tps, 0, 0)),
            pl.BlockSpec((tm, tn), lambda i, j: (back(i), j)),
            pl.BlockSpec((nb, 1, tn), lambda i, j: (row0 // nb + back(i) // tps, 0, GATE1 * nj + j)),
            pl.BlockSpec((NCHUNK, CONV_W, LANES), lambda i, j: (0, 0, 0)),
            pl.BlockSpec((NCHUNK, 1, LANES), lambda i, j: (0, 0, 0)),
            pl.BlockSpec((1, D_MODEL), lambda i, j: (0, 0)),
            pl.BlockSpec((1, D_MODEL), lambda i, j: (0, 0)),
            pl.BlockSpec((D_MODEL, tn), lambda i, j: (0, j)),
            pl.BlockSpec((1, tn), lambda i, j: (0, j)),
        ] + cast_specs,
        out_specs=[pl.BlockSpec((tm, tn), lambda i, j: (back(i), j * jnp.minimum(i, 1)))] + cast_specs,
        out_shape=[jax.ShapeDtypeStruct((rows, D_MODEL), F32)]
        + [jax.ShapeDtypeStruct(a.shape, BF16) for a in casts],
        scratch_shapes=[
            pltpu.VMEM((nb * NCHUNK, HALO + tt, LANES), F32),
            pltpu.VMEM((NCHUNK, tm, LANES), F32),
            pltpu.VMEM((2, tm, D_MODEL), BF16),
        ],
        compiler_params=_params(("arbitrary", "arbitrary")),
        name="conv_pw2",
    )(g2, g2, prev, x2, mod,
      w_dw.reshape(CONV_W, NCHUNK, LANES).transpose(1, 0, 2), b_dw.reshape(NCHUNK, 1, LANES),
      ln_g, ln_b, w2, b2, *casts)
    return outs[0], list(outs[1:])


FFN_SUB = 512


def _ffn_kernel(x_ref, gain_ref, scale_ref, shift_ref, gate_ref, wa_ref, wb_ref, wo_ref, fin_ref,
                o_ref, h_a, h_b, *, nb, final_norm):
    i = pl.program_id(0)
    f = pl.program_id(1)
    tm = x_ref.shape[0]
    rows = tm // nb

    @pl.when(f == 0)
    def _():
        @pl.when(i == 0)
        def _():
            _fill_h(h_a, x_ref, gain_ref, scale_ref, shift_ref, nb)

        o_ref[...] = x_ref[...]

    slab = jnp.clip(f - 1, 0, tm // FILL_ROWS - 1)

    def body(h_cur, h_next):
        _fill_rows(h_next, x_ref, gain_ref, scale_ref, shift_ref, nb, slab * FILL_ROWS, FILL_ROWS)
        for m in range(tm // FFN_SUB):
            h = h_cur[m * FFN_SUB:(m + 1) * FFN_SUB, :]
            a = jnp.dot(h, wa_ref[...], preferred_element_type=F32)
            b = jnp.dot(h, wb_ref[...], preferred_element_type=F32)
            act = (_silu(a) * b).astype(BF16)
            res = jnp.dot(act, wo_ref[...], preferred_element_type=F32)
            piece = min(rows, FFN_SUB)
            for u in range(FFN_SUB // piece):
                lo = m * FFN_SUB + u * piece
                o_ref[lo:lo + piece, :] += gate_ref[lo // rows] * res[u * piece:(u + 1) * piece, :]

    _by_parity(i, h_a, h_b, body)

    if final_norm:
        @pl.when(f == pl.num_programs(1) - 1)
        def _():
            def body(r, carry):
                rs = pl.ds(pl.multiple_of(r * FILL_ROWS, FILL_ROWS), FILL_ROWS)
                y = o_ref[rs, :]
                o_ref[rs, :] = y * lax.rsqrt(jnp.mean(y * y, axis=-1, keepdims=True) + EPS) * fin_ref[...]
                return carry

            lax.fori_loop(0, tm // FILL_ROWS, body, 0)


def _ffn(x2, seq_len, gain, mod, row0, w_in, w_out, layer, fin_gain, final_norm, tm=1024, tf=512):
    rows = x2.shape[0]
    nb, tps = _row_tiling(seq_len, tm)
    nf = D_FF // tf
    n_tiles = rows // tm
    assert nf - 1 >= tm // FILL_ROWS
    vec_spec = pl.BlockSpec((1, D_MODEL), lambda i, f: (0, 0))
    x_tile = lambda i, f: jnp.minimum(i + jnp.minimum(f, 1), n_tiles - 1)
    return pl.pallas_call(
        functools.partial(_ffn_kernel, nb=nb, final_norm=final_norm),
        grid=(n_tiles, nf),
        in_specs=[
            pl.BlockSpec((tm, D_MODEL), lambda i, f: (x_tile(i, f), 0), pipeline_mode=pl.Buffered(1)),
            vec_spec,
            _mod_spec(SCALE2, row0, nb, tps, tile=x_tile), _mod_spec(SHIFT2, row0, nb, tps, tile=x_tile),
            _mod_spec(GATE2, row0, nb, tps),
            pl.BlockSpec((None, D_MODEL, tf), lambda i, f: (layer, 0, f)),
            pl.BlockSpec((None, D_MODEL, tf), lambda i, f: (layer, 0, nf + f)),
            pl.BlockSpec((None, tf, D_MODEL), lambda i, f: (layer, f, 0)),
            vec_spec,
        ],
        out_specs=pl.BlockSpec((tm, D_MODEL), lambda i, f: (i, 0)),
        out_shape=jax.ShapeDtypeStruct((rows, D_MODEL), F32),
        scratch_shapes=[pltpu.VMEM((tm, D_MODEL), BF16), pltpu.VMEM((tm, D_MODEL), BF16)],
        compiler_params=_params(("arbitrary", "arbitrary")),
        name="ffn",
    )(x2, gain, mod, mod, mod, w_in, w_in, w_out, fin_gain)


def _qkv_kernel(x_ref, gain_ref, scale_ref, shift_ref, wq_ref, wkv_ref, q_ref, kv_ref, h_scr, *, nb):
    @pl.when(pl.program_id(1) == 0)
    def _():
        _fill_h(h_scr, x_ref, gain_ref, scale_ref, shift_ref, nb)
        kv_ref[...] = jnp.dot(h_scr[...], wkv_ref[...], preferred_element_type=F32)

    q = jnp.dot(h_scr[...], wq_ref[...], preferred_element_type=F32)
    q_ref[...] = (q * (HEAD_DIM ** -0.5)).astype(BF16)


def _qkv(x2, seq_len, gain, mod, row0, w_qkv, tm=1024, tn=512):
    rows = x2.shape[0]
    nb, tps = _row_tiling(seq_len, tm)
    nq = (N_HEADS * HEAD_DIM) // tn
    kv_cols = 2 * N_KV * HEAD_DIM
    assert kv_cols == tn
    return pl.pallas_call(
        functools.partial(_qkv_kernel, nb=nb),
        grid=(rows // tm, nq),
        in_specs=[
            pl.BlockSpec((tm, D_MODEL), lambda i, j: (i, 0)),
            pl.BlockSpec((1, D_MODEL), lambda i, j: (0, 0)),
            _mod_spec(SCALE1, row0, nb, tps), _mod_spec(SHIFT1, row0, nb, tps),
            pl.BlockSpec((D_MODEL, tn), lambda i, j: (0, j)),
            pl.BlockSpec((D_MODEL, kv_cols), lambda i, j: (0, nq)),
        ],
        out_specs=[
            pl.BlockSpec((tm, tn), lambda i, j: (i, j)),
            pl.BlockSpec((tm, kv_cols), lambda i, j: (i, 0)),
        ],
        out_shape=[
            jax.ShapeDtypeStruct((rows, N_HEADS * HEAD_DIM), BF16),
            jax.ShapeDtypeStruct((rows, kv_cols), F32),
        ],
        scratch_shapes=[pltpu.VMEM((tm, D_MODEL), BF16)],
        compiler_params=_params(("arbitrary", "arbitrary")),
        name="qkv",
    )(x2, gain, mod, mod, w_qkv, w_qkv)


PAIRS_PER_KV = GROUP // 2
QROWS = PAIRS_PER_KV * CHUNK
BAND = WINDOW + CHUNK
ATT_G = 8
KVW = N_KV * HEAD_DIM


def _attn_kernel(slope_ref, sink_ref, q_ref, *refs, band, steps_per_seq):
    if band:
        kp2_ref, kp1_ref, kc_ref, vp2_ref, vp1_ref, vc_ref = refs[:6]
        refs = refs[6:]
    else:
        ck_ref, kn_ref, cv_ref, vn_ref = refs[:4]
        refs = refs[4:]
    o_ref, bias_scr, ones_scr, kl_scr, kr_scr, vl_scr, vr_scr = refs
    step = pl.program_id(0)

    @pl.when(step == 0)
    def _():
        row = lax.broadcasted_iota(jnp.int32, (QROWS, 2 * KPAD), 0)
        col = lax.broadcasted_iota(jnp.int32, (QROWS, 2 * KPAD), 1)
        key = jnp.bitwise_and(col, KPAD - 1)
        dist = jnp.abs(WINDOW + jnp.bitwise_and(row, CHUNK - 1) - key).astype(F32)
        for kv in range(N_KV):
            slope = jnp.zeros((QROWS, 2 * KPAD), F32)
            sink = jnp.zeros((QROWS, 2 * KPAD), F32)
            for pr in range(PAIRS_PER_KV):
                in_pair = jnp.logical_and(row >= pr * CHUNK, row < (pr + 1) * CHUNK)
                in_a = jnp.logical_and(in_pair, col < KPAD)
                in_b = jnp.logical_and(in_pair, col >= KPAD)
                head = kv * GROUP + pr * 2
                slope = jnp.where(in_a, slope_ref[head], jnp.where(in_b, slope_ref[head + 1], slope))
                sink = jnp.where(in_a, sink_ref[head], jnp.where(in_b, sink_ref[head + 1], sink))
            bias_scr[kv] = jnp.where(key < BAND, -(slope * dist), jnp.where(key == BAND, sink, NEG))
        orow = lax.broadcasted_iota(jnp.int32, (2 * KPAD, PAIR), 0)
        ocol = lax.broadcasted_iota(jnp.int32, (2 * KPAD, PAIR), 1)
        in_first = ocol < HEAD_DIM
        ones_scr[...] = jnp.where(orow < KPAD, jnp.where(in_first, 1.0, 0.0),
                                  jnp.where(in_first, 0.0, 1.0)).astype(BF16)

    lane64 = lax.broadcasted_iota(jnp.int32, (CHUNK, PAIR), 1) < HEAD_DIM

    def prep(src_ref, r_src, col0, r_dst, l_scr, r_scr):
        for lb in range(N_KV // 2):
            blk = src_ref[r_src:r_src + CHUNK, col0 + lb * PAIR:col0 + (lb + 1) * PAIR]
            ev = jnp.where(lane64, blk, 0.0)
            od = jnp.where(lane64, 0.0, blk)
            ds = slice(r_dst, r_dst + CHUNK)
            l_scr[2 * lb, ds, :] = ev.astype(BF16)
            r_scr[2 * lb, ds, :] = pltpu.roll(ev, HEAD_DIM, axis=1).astype(BF16)
            r_scr[2 * lb + 1, ds, :] = od.astype(BF16)
            l_scr[2 * lb + 1, ds, :] = pltpu.roll(od, HEAD_DIM, axis=1).astype(BF16)

    if band:
        prep(kp2_ref, 0, 0, 0, kl_scr, kr_scr)
        prep(kp1_ref, 0, 0, CHUNK, kl_scr, kr_scr)
        prep(vp2_ref, 0, 0, 0, vl_scr, vr_scr)
        prep(vp1_ref, 0, 0, CHUNK, vl_scr, vr_scr)
        for g in range(ATT_G):
            prep(kc_ref, g * CHUNK, 0, WINDOW + g * CHUNK, kl_scr, kr_scr)
            prep(vc_ref, g * CHUNK, 0, WINDOW + g * CHUNK, vl_scr, vr_scr)
    else:
        for g in range(ATT_G):
            for h in range(WINDOW // CHUNK):
                prep(ck_ref, g * WINDOW + h * CHUNK, 0, g * BAND + h * CHUNK, kl_scr, kr_scr)
                prep(cv_ref, g * WINDOW + h * CHUNK, 0, g * BAND + h * CHUNK, vl_scr, vr_scr)
            prep(kn_ref, g * CHUNK, 0, g * BAND + WINDOW, kl_scr, kr_scr)
            prep(vn_ref, g * CHUNK, 0, g * BAND + WINDOW, vl_scr, vr_scr)

    zpad = jnp.zeros((KPAD - BAND, PAIR), BF16)
    keyrow = jnp.bitwise_and(lax.broadcasted_iota(jnp.int32, (1, 2 * KPAD), 1), KPAD - 1)
    n0 = (step % steps_per_seq) * ATT_G

    def chunk(c, carry, masked):
        r0 = pl.multiple_of(c * CHUNK, CHUNK)
        k0 = r0 if band else pl.multiple_of(c * BAND, CHUNK)
        ks = pl.ds(k0, BAND)
        qrows = pl.ds(r0, CHUNK)
        if masked:
            maskrow = jnp.where(keyrow >= WINDOW - (n0 + c) * CHUNK, 0.0, NEG)
        for kv in range(N_KV):
            qs = jnp.concatenate(
                [q_ref[qrows, kv * GROUP * HEAD_DIM + pr * PAIR: kv * GROUP * HEAD_DIM + (pr + 1) * PAIR]
                 for pr in range(PAIRS_PER_KV)], axis=0)
            kbd = jnp.concatenate([kl_scr[kv, ks, :], zpad, kr_scr[kv, ks, :], zpad], axis=0)
            vbd = jnp.concatenate([vl_scr[kv, ks, :], zpad, vr_scr[kv, ks, :], zpad], axis=0)
            s = lax.dot_general(qs, kbd, (((1,), (1,)), ((), ())), preferred_element_type=F32)
            s = s + bias_scr[kv]
            if masked:
                s = s + maskrow
            s_a = s[:, :KPAD]
            s_b = s[:, KPAD:]
            m_a = jnp.max(s_a, axis=-1, keepdims=True)
            m_b = jnp.max(s_b, axis=-1, keepdims=True)
            p = jnp.concatenate([jnp.exp(s_a - m_a), jnp.exp(s_b - m_b)], axis=1).astype(BF16)
            ol = jnp.dot(p, jnp.concatenate([vbd, ones_scr[...]], axis=1), preferred_element_type=F32)
            o = ol[:, :PAIR] / ol[:, PAIR:]
            for pr in range(PAIRS_PER_KV):
                c0 = kv * GROUP * HEAD_DIM + pr * PAIR
                o_ref[qrows, c0:c0 + PAIR] = o[pr * CHUNK:(pr + 1) * CHUNK, :].astype(BF16)
        return carry

    n_lead = WINDOW // CHUNK if band else 0
    if n_lead:
        lax.fori_loop(0, n_lead, functools.partial(chunk, masked=True), 0, unroll=2)
    lax.fori_loop(n_lead, ATT_G, functools.partial(chunk, masked=False), 0, unroll=2)


def _attention(q, kv, cache_k, cache_v, slopes, sinks, seq_len):
    rows = q.shape[0]
    tq = ATT_G * CHUNK
    band = cache_k is None
    smem = pl.BlockSpec(memory_space=pltpu.SMEM)
    if band:
        assert seq_len % tq == 0
        sps = seq_len // tq
        cps = seq_len // CHUNK

        def prev(d, col):
            return pl.BlockSpec((CHUNK, KVW), lambda i: (jnp.maximum((i % sps) * ATT_G - d, 0) + (i // sps) * cps, col))

        kv_specs = [prev(2, 0), prev(1, 0), pl.BlockSpec((tq, KVW), lambda i: (i, 0)),
                    prev(2, 1), prev(1, 1), pl.BlockSpec((tq, KVW), lambda i: (i, 1))]
        kv_args = [kv] * 6
        key_rows = WINDOW + tq
    else:
        assert seq_len == CHUNK
        sps = 1
        kv_specs = [pl.BlockSpec((ATT_G * WINDOW, KVW), lambda i: (i, 0)), pl.BlockSpec((tq, KVW), lambda i: (i, 0)),
                    pl.BlockSpec((ATT_G * WINDOW, KVW), lambda i: (i, 0)), pl.BlockSpec((tq, KVW), lambda i: (i, 1))]
        kv_args = [cache_k, kv, cache_v, kv]
        key_rows = ATT_G * BAND
    side = pltpu.VMEM((N_KV, key_rows, PAIR), BF16)
    return pl.pallas_call(
        functools.partial(_attn_kernel, band=band, steps_per_seq=sps),
        grid=(rows // tq,),
        in_specs=[smem, smem, pl.BlockSpec((tq, N_HEADS * HEAD_DIM), lambda i: (i, 0))] + kv_specs,
        out_specs=pl.BlockSpec((tq, N_HEADS * HEAD_DIM), lambda i: (i, 0)),
        out_shape=jax.ShapeDtypeStruct((rows, N_HEADS * HEAD_DIM), BF16),
        scratch_shapes=[
            pltpu.VMEM((N_KV, QROWS, 2 * KPAD), F32),
            pltpu.VMEM((2 * KPAD, PAIR), BF16),
            side, side, side, side,
        ],
        compiler_params=_params(("arbitrary",)),
        name="attention",
    )(slopes, sinks, q, *kv_args)


def _wo_kernel(a_ref, w_ref, x_ref, gate_ref, o_ref, *, nb):
    out = jnp.dot(a_ref[...], w_ref[...], preferred_element_type=F32)
    rows = a_ref.shape[0] // nb
    for s in range(nb):
        sl = slice(s * rows, (s + 1) * rows)
        o_ref[sl, :] = x_ref[sl, :] + gate_ref[s] * out[sl, :]


def _wo(attn, x2, seq_len, mod, row0, w_o, tm=1024, tn=1024):
    rows = x2.shape[0]
    nb, tps = _row_tiling(seq_len, tm)
    return pl.pallas_call(
        functools.partial(_wo_kernel, nb=nb),
        grid=(rows // tm, D_MODEL // tn),
        in_specs=[
            pl.BlockSpec((tm, N_HEADS * HEAD_DIM), lambda i, j: (i, 0)),
            pl.BlockSpec((N_HEADS * HEAD_DIM, tn), lambda i, j: (0, j)),
            pl.BlockSpec((tm, tn), lambda i, j: (i, j)),
            _mod_spec(GATE1, row0, nb, tps, width=tn),
        ],
        out_specs=pl.BlockSpec((tm, tn), lambda i, j: (i, j)),
        out_shape=jax.ShapeDtypeStruct((rows, D_MODEL), F32),
        compiler_params=_params(("arbitrary", "arbitrary")),
        name="wo",
    )(attn, w_o, x2, mod)


def _trunk(x, mods, row0, state_conv, cache_k, cache_v, wts, late):
    batch, seq_len, _ = x.shape
    rows = batch * seq_len
    x2 = x.reshape(rows, D_MODEL)
    prompt = state_conv is None
    row = lambda v: v.reshape(1, -1)
    assert seq_len >= CONV_W - 1 and seq_len >= WINDOW // 2

    g = _pw1_glu(x2, seq_len, row(wts["norm_mix_g"][0]), mods[0], row0, wts["w_pw1"], row(wts["b_pw1"][0]))
    if prompt:
        prev = jnp.zeros((batch, HALO, D_MODEL), F32)
    else:
        prev = jnp.pad(state_conv[0], ((0, 0), (HALO - (CONV_W - 1), 0), (0, 0)))
    conv_new = g.reshape(batch, seq_len, D_MODEL)[:, -(CONV_W - 1):][None]
    casts = ()
    if late is None:
        casts = (wts["w_ffn_in"].reshape(DEPTH * D_MODEL, 2 * D_FF), wts["w_ffn_out"].reshape(DEPTH * D_FF, D_MODEL),
                 wts["w_qkv"][0], wts["w_o"][0])
    x2, cast_out = _conv_pw2(g, prev, x2, seq_len, mods[0], row0, wts["w_dw"][0], row(wts["b_dw"][0]),
                             row(wts["ln_g"][0]), row(wts["ln_b"][0]), wts["w_pw2"], row(wts["b_pw2"][0]), casts)
    if late is None:
        late = dict(w_ffn_in=cast_out[0].reshape(DEPTH, D_MODEL, 2 * D_FF),
                    w_ffn_out=cast_out[1].reshape(DEPTH, D_FF, D_MODEL), w_qkv=cast_out[2], w_o=cast_out[3])
    x2 = _ffn(x2, seq_len, row(wts["norm_ffn_g"][0]), mods[0], row0, late["w_ffn_in"], late["w_ffn_out"], 0,
              row(wts["final_norm_g"]), final_norm=False)

    q, kv = _qkv(x2, seq_len, row(wts["norm_mix_g"][1]), mods[1], row0, late["w_qkv"])
    kv3 = kv.reshape(batch, seq_len, 2 * KVW)
    if prompt:
        attn = _attention(q, kv, None, None, wts["slopes"], wts["sinks"], seq_len)
        k_new = kv3[:, -WINDOW:, :KVW]
        v_new = kv3[:, -WINDOW:, KVW:]
    else:
        ck = cache_k[0].reshape(batch * WINDOW, KVW)
        cv = cache_v[0].reshape(batch * WINDOW, KVW)
        attn = _attention(q, kv, ck, cv, wts["slopes"], wts["sinks"], seq_len)
        keep = WINDOW - seq_len
        k_new = jnp.concatenate([cache_k[0].reshape(batch, WINDOW, KVW)[:, -keep:], kv3[:, :, :KVW]], axis=1)
        v_new = jnp.concatenate([cache_v[0].reshape(batch, WINDOW, KVW)[:, -keep:], kv3[:, :, KVW:]], axis=1)
    x2 = _wo(attn, x2, seq_len, mods[1], row0, late["w_o"])
    y2 = _ffn(x2, seq_len, row(wts["norm_ffn_g"][1]), mods[1], row0, late["w_ffn_in"], late["w_ffn_out"], 1,
              row(wts["final_norm_g"]), final_norm=True)

    shape_kv = (1, batch, WINDOW, N_KV, HEAD_DIM)
    return (y2.reshape(batch, seq_len, D_MODEL), conv_new,
            k_new.reshape(shape_kv), v_new.reshape(shape_kv)), late


def kernel(x_prompt, x_sample, c_prompt, c_sample, state_conv, cache_k, cache_v, norm_mix_g, norm_ffn_g,
           w_ada, b_ada, w_pw1, b_pw1, w_dw, b_dw, ln_g, ln_b, w_pw2, b_pw2, w_qkv, w_o, sinks,
           w_ffn_in, w_ffn_out, final_norm_g):
    bp, bs = c_prompt.shape[0], c_sample.shape[0]
    b_pad = -(-(bs + bp) // 8) * 8
    c_all = jnp.concatenate([c_sample, c_prompt, jnp.zeros((b_pad - bp - bs, D_MODEL), F32)], axis=0)
    mod = _adaln(c_all, w_ada, b_ada)
    mods = [mod[l].reshape(b_pad, 1, 6 * D_MODEL) for l in range(DEPTH)]

    heads = jnp.arange(1, N_HEADS + 1, dtype=F32)
    wts = dict(
        norm_mix_g=norm_mix_g, norm_ffn_g=norm_ffn_g, final_norm_g=final_norm_g,
        w_pw1=w_pw1[0].astype(BF16), b_pw1=b_pw1, w_dw=w_dw, b_dw=b_dw, ln_g=ln_g, ln_b=ln_b,
        w_pw2=w_pw2[0].astype(BF16), b_pw2=b_pw2, w_qkv=w_qkv, w_o=w_o, w_ffn_in=w_ffn_in, w_ffn_out=w_ffn_out,
        slopes=jnp.exp2(-8.0 * heads / N_HEADS), sinks=sinks[0].astype(F32),
    )
    (y_p, conv_p, k_p, v_p), late = _trunk(x_prompt, mods, bs, None, None, None, wts, None)
    (y_s, conv_s, k_s, v_s), _ = _trunk(x_sample, mods, 0, state_conv, cache_k, cache_v, wts, late)
    return (y_p, y_s, conv_p, conv_s, k_p, v_p, k_s, v_s)
```

```python
import functools

import jax
import jax.numpy as jnp
from jax import lax
from jax.experimental import pallas as pl
from jax.experimental.pallas import tpu as pltpu

F32 = jnp.float32
BF16 = jnp.bfloat16

D_MODEL = 2048
DEPTH = 2
CHUNK = 64
CONV_W = 31
N_HEADS = 32
N_KV = 4
HEAD_DIM = 64
GROUP = N_HEADS // N_KV
WINDOW = 128
D_FF = 5632
EPS = 1e-6
NEG = -1e30

VMEM_LIMIT_BYTES = 60 * 1024 * 1024
LANES = 128
SUBLANES = 8
HALO = 32
PAIR = 2 * HEAD_DIM
KPAD = 256

SHIFT1, SCALE1, GATE1, SHIFT2, SCALE2, GATE2 = range(6)


def _params(sem):
    return pltpu.CompilerParams(dimension_semantics=sem, vmem_limit_bytes=VMEM_LIMIT_BYTES)


def _silu(v):
    return v * jax.nn.sigmoid(v)


FILL_ROWS = 128


def _fill_rows(h_ref, x_ref, gain_ref, scale_ref, shift_ref, nb, start, n_rows):
    rows = x_ref.shape[0] // nb
    blk = min(rows, FILL_ROWS)
    for u in range(n_rows // blk):
        at = start + u * blk
        s = 0 if nb == 1 else at // rows
        rs = pl.ds(pl.multiple_of(at, blk), blk)
        x = x_ref[rs, :]
        y = x * lax.rsqrt(jnp.mean(x * x, axis=-1, keepdims=True) + EPS)
        mul = gain_ref[...] * (1.0 + scale_ref[s])
        h_ref[rs, :] = (y * mul + shift_ref[s]).astype(BF16)


def _fill_h(h_ref, x_ref, gain_ref, scale_ref, shift_ref, nb):
    def body(r, carry):
        _fill_rows(h_ref, x_ref, gain_ref, scale_ref, shift_ref, nb, r * FILL_ROWS, FILL_ROWS)
        return carry

    lax.fori_loop(0, x_ref.shape[0] // FILL_ROWS, body, 0)


def _by_parity(i, h_a, h_b, body):
    @pl.when(i % 2 == 0)
    def _():
        body(h_a, h_b)

    @pl.when(i % 2 == 1)
    def _():
        body(h_b, h_a)


def _adaln_kernel(c_ref, w_ref, b_ref, o_ref):
    c = c_ref[...]
    a = _silu(c).astype(BF16)
    o_ref[0] = jnp.dot(a, w_ref[0].astype(BF16), preferred_element_type=F32) + b_ref[0]


def _adaln(c_all, w_ada, b_ada, tn=1024):
    bp = c_all.shape[0]
    n = w_ada.shape[-1]
    return pl.pallas_call(
        _adaln_kernel,
        grid=(DEPTH, n // tn),
        in_specs=[
            pl.BlockSpec((bp, D_MODEL), lambda l, j: (0, 0)),
            pl.BlockSpec((1, D_MODEL, tn), lambda l, j: (l, 0, j)),
            pl.BlockSpec((1, 1, tn), lambda l, j: (l, 0, j)),
        ],
        out_specs=pl.BlockSpec((1, bp, tn), lambda l, j: (l, 0, j)),
        out_shape=jax.ShapeDtypeStruct((DEPTH, bp, n), F32),
        compiler_params=_params(("arbitrary", "arbitrary")),
        name="adaln",
    )(c_all, w_ada, b_ada.reshape(DEPTH, 1, n))


def _row_tiling(seq_len, tm):
    if tm >= seq_len:
        assert tm % seq_len == 0
        return tm // seq_len, 1
    assert seq_len % tm == 0
    return 1, seq_len // tm


def _mod_spec(which, row0, nb, tps, width=D_MODEL, tile=lambda i, j: i):
    assert row0 % nb == 0
    per = D_MODEL // width
    if per == 1:
        return pl.BlockSpec((nb, 1, width), lambda i, j: (row0 // nb + tile(i, j) // tps, 0, which))
    return pl.BlockSpec((nb, 1, width), lambda i, j: (row0 // nb + tile(i, j) // tps, 0, which * per + j))


def _pw1_glu_kernel(x_ref, gain_ref, scale_ref, shift_ref, wa_ref, wb_ref, ba_ref, bb_ref,
                    g_ref, h_scr, *, nb):
    @pl.when(pl.program_id(1) == 0)
    def _():
        _fill_h(h_scr, x_ref, gain_ref, scale_ref, shift_ref, nb)

    h = h_scr[...]
    a = jnp.dot(h, wa_ref[...], preferred_element_type=F32) + ba_ref[...]
    b = jnp.dot(h, wb_ref[...], preferred_element_type=F32) + bb_ref[...]
    g_ref[...] = a * jax.nn.sigmoid(b)


def _pw1_glu(x2, seq_len, gain, mod, row0, w1, b1, tm=1024, tn=512):
    rows = x2.shape[0]
    nb, tps = _row_tiling(seq_len, tm)
    nj = D_MODEL // tn
    return pl.pallas_call(
        functools.partial(_pw1_glu_kernel, nb=nb),
        grid=(rows // tm, nj),
        in_specs=[
            pl.BlockSpec((tm, D_MODEL), lambda i, j: (i, 0)),
            pl.BlockSpec((1, D_MODEL), lambda i, j: (0, 0)),
            _mod_spec(SCALE1, row0, nb, tps), _mod_spec(SHIFT1, row0, nb, tps),
            pl.BlockSpec((D_MODEL, tn), lambda i, j: (0, j)),
            pl.BlockSpec((D_MODEL, tn), lambda i, j: (0, nj + j)),
            pl.BlockSpec((1, tn), lambda i, j: (0, j)),
            pl.BlockSpec((1, tn), lambda i, j: (0, nj + j)),
        ],
        out_specs=pl.BlockSpec((tm, tn), lambda i, j: (i, j)),
        out_shape=jax.ShapeDtypeStruct((rows, D_MODEL), F32),
        scratch_shapes=[pltpu.VMEM((tm, D_MODEL), BF16)],
        compiler_params=_params(("arbitrary", "arbitrary")),
        name="pw1_glu",
    )(x2, gain, mod, mod, w1, w1, b1, b1)


CONV_ROWS = 64
LN_ROWS = 128
NCHUNK = D_MODEL // LANES


def _conv_pw2_kernel(g_ref, halo_ref, prev_ref, x_ref, gate_ref, wdw_ref, bdw_ref, lng_ref, lnb_ref,
                     w2_ref, b2_ref, *rest, nb, tps, n_cast):
    cast_src = rest[:n_cast]
    o_ref = rest[n_cast]
    cast_dst = rest[n_cast + 1:2 * n_cast + 1]
    win_scr, shift_scr, y_scr, ybf_scr = rest[2 * n_cast + 1:]
    i = pl.program_id(0)
    j = pl.program_id(1)
    nj = pl.num_programs(1)
    tm = g_ref.shape[0]
    tt = tm // nb
    ti = jnp.minimum(i, pl.num_programs(0) - 2)
    cur = i % 2
    chunks_per_step = w2_ref.shape[1] // LANES

    for src, dst in zip(cast_src, cast_dst):
        dst[...] = src[...].astype(BF16)

    @pl.when(j == 0)
    def _():
        @pl.when(i == 0)
        def _():
            ybf_scr[1] = jnp.zeros(ybf_scr.shape[1:], BF16)

        def put_history(src_of):
            for s in range(nb):
                for c in range(NCHUNK):
                    win_scr[s * NCHUNK + c, 0:HALO, :] = src_of(s, c)

        if tps == 1:
            put_history(lambda s, c: prev_ref[s, :, c * LANES:(c + 1) * LANES])
        else:
            first = (ti % tps) == 0

            @pl.when(first)
            def _():
                put_history(lambda s, c: prev_ref[s, :, c * LANES:(c + 1) * LANES])

            @pl.when(jnp.logical_not(first))
            def _():
                put_history(lambda s, c: halo_ref[:, c * LANES:(c + 1) * LANES])
        for s in range(nb):
            for c in range(NCHUNK):
                win_scr[s * NCHUNK + c, HALO:HALO + tt, :] = g_ref[s * tt:(s + 1) * tt, c * LANES:(c + 1) * LANES]

    lead = HALO - (CONV_W - 1)

    def conv_and_matmul(jj):
        span = tt + HALO - SUBLANES
        for cc in range(chunks_per_step):
            c = jj * chunks_per_step + cc
            for s in range(nb):
                for f in range(1, SUBLANES):
                    shift_scr[cc * nb + s, f - 1, 0:span, :] = win_scr[s * NCHUNK + c, f:f + span, :]
            for s in range(nb):
                for rb in range(tt // CONV_ROWS):
                    acc = jnp.zeros((CONV_ROWS, LANES), F32) + bdw_ref[c]
                    for k in range(CONV_W):
                        f = (k + lead) % SUBLANES
                        r0 = rb * CONV_ROWS + (k + lead) - f
                        if f == 0:
                            tap = win_scr[s * NCHUNK + c, r0:r0 + CONV_ROWS, :]
                        else:
                            tap = shift_scr[cc * nb + s, f - 1, r0:r0 + CONV_ROWS, :]
                        acc = acc + tap * wdw_ref[c, k:k + 1, :]
                    y_scr[c, s * tt + rb * CONV_ROWS:s * tt + (rb + 1) * CONV_ROWS, :] = acc

        out = jnp.dot(ybf_scr[1 - cur], w2_ref[...], preferred_element_type=F32) + b2_ref[...]
        for s in range(nb):
            sl = slice(s * tt, (s + 1) * tt)
            o_ref[sl, :] = x_ref[sl, :] + gate_ref[s] * out[sl, :]

    for jj in range(NCHUNK // chunks_per_step):
        pl.when(j == jj)(functools.partial(conv_and_matmul, jj))

    @pl.when(j == nj - 1)
    def _():
        def ln_block(r, carry):
            rs = pl.ds(pl.multiple_of(r * LN_ROWS, LN_ROWS), LN_ROWS)
            tot = y_scr[0, rs, :]
            for c in range(1, NCHUNK):
                tot = tot + y_scr[c, rs, :]
            mu = jnp.sum(tot, axis=-1, keepdims=True) * (1.0 / D_MODEL)
            sq = jnp.zeros((LN_ROWS, LANES), F32)
            for c in range(NCHUNK):
                yc = y_scr[c, rs, :] - mu
                sq = sq + yc * yc
            rstd = lax.rsqrt(jnp.sum(sq, axis=-1, keepdims=True) * (1.0 / D_MODEL) + EPS)
            for c in range(NCHUNK):
                cs = slice(c * LANES, (c + 1) * LANES)
                z = (y_scr[c, rs, :] - mu) * rstd * lng_ref[:, cs] + lnb_ref[:, cs]
                ybf_scr[cur, rs, cs] = _silu(z).astype(BF16)
            return carry

        lax.fori_loop(0, tm // LN_ROWS, ln_block, 0)


def _conv_pw2(g2, prev, x2, seq_len, mod, row0, w_dw, b_dw, ln_g, ln_b, w2, b2, casts=(), tm=512, tn=512):
    rows = g2.shape[0]
    nb, tps = _row_tiling(seq_len, tm)
    tt = tm // nb
    hb = tm // HALO
    nj = D_MODEL // tn
    n_tiles = rows // tm
    n_steps = n_tiles * nj
    cast_specs = []
    for a in casts:
        slab = a.shape[0] // n_steps
        assert a.shape[0] % n_steps == 0 and slab % 16 == 0, a.shape
        cast_specs.append(pl.BlockSpec((slab, a.shape[1]), lambda i, j: (jnp.minimum(i * nj + j, n_steps - 1), 0)))
    front = lambda i: jnp.minimum(i, n_tiles - 1)
    back = lambda i: jnp.maximum(i - 1, 0)
    assert row0 % nb == 0
    outs = pl.pallas_call(
        functools.partial(_conv_pw2_kernel, nb=nb, tps=tps, n_cast=len(casts)),
        grid=(n_tiles + 1, nj),
        in_specs=[
            pl.BlockSpec((tm, D_MODEL), lambda i, j: (front(i), 0)),
            pl.BlockSpec((HALO, D_MODEL), lambda i, j: (jnp.maximum(front(i) * hb - 1, 0), 0)),
            pl.BlockSpec((nb, HALO, D_MODEL), lambda i, j: (front(i) // tps, 0, 0)),
            pl.BlockSpec((tm, tn), lambda i, j: (back(i), j)),
            pl.BlockSpec((nb, 1, tn), lambda i, j: (row0 // nb + back(i) // tps, 0, GATE1 * nj + j)),
            pl.BlockSpec((NCHUNK, CONV_W, LANES), lambda i, j: (0, 0, 0)),
            pl.BlockSpec((NCHUNK, 1, LANES), lambda i, j: (0, 0, 0)),
            pl.BlockSpec((1, D_MODEL), lambda i, j: (0, 0)),
            pl.BlockSpec((1, D_MODEL), lambda i, j: (0, 0)),
            pl.BlockSpec((D_MODEL, tn), lambda i, j: (0, j)),
            pl.BlockSpec((1, tn), lambda i, j: (0, j)),
        ] + cast_specs,
        out_specs=[pl.BlockSpec((tm, tn), lambda i, j: (back(i), j * jnp.minimum(i, 1)))] + cast_specs,
        out_shape=[jax.ShapeDtypeStruct((rows, D_MODEL), F32)]
        + [jax.ShapeDtypeStruct(a.shape, BF16) for a in casts],
        scratch_shapes=[
            pltpu.VMEM((nb * NCHUNK, HALO + tt, LANES), F32),
            pltpu.VMEM(((tn // LANES) * nb, SUBLANES - 1, HALO + tt, LANES), F32),
            pltpu.VMEM((NCHUNK, tm, LANES), F32),
            pltpu.VMEM((2, tm, D_MODEL), BF16),
        ],
        compiler_params=_params(("arbitrary", "arbitrary")),
        name="conv_pw2",
    )(g2, g2, prev, x2, mod,
      w_dw.reshape(CONV_W, NCHUNK, LANES).transpose(1, 0, 2), b_dw.reshape(NCHUNK, 1, LANES),
      ln_g, ln_b, w2, b2, *casts)
    return outs[0], list(outs[1:])


FFN_SUB = 512


def _ffn_kernel(x_ref, gain_ref, scale_ref, shift_ref, gate_ref, wa_ref, wb_ref, wo_ref, fin_ref,
                o_ref, h_a, h_b, *, nb, final_norm):
    i = pl.program_id(0)
    f = pl.program_id(1)
    tm = x_ref.shape[0]
    rows = tm // nb

    @pl.when(f == 0)
    def _():
        @pl.when(i == 0)
        def _():
            _fill_h(h_a, x_ref, gain_ref, scale_ref, shift_ref, nb)

        o_ref[...] = x_ref[...]

    slab = jnp.clip(f - 1, 0, tm // FILL_ROWS - 1)

    def body(h_cur, h_next):
        _fill_rows(h_next, x_ref, gain_ref, scale_ref, shift_ref, nb, slab * FILL_ROWS, FILL_ROWS)
        for m in range(tm // FFN_SUB):
            h = h_cur[m * FFN_SUB:(m + 1) * FFN_SUB, :]
            a = jnp.dot(h, wa_ref[...], preferred_element_type=F32)
            b = jnp.dot(h, wb_ref[...], preferred_element_type=F32)
            act = (_silu(a) * b).astype(BF16)
            res = jnp.dot(act, wo_ref[...], preferred_element_type=F32)
            piece = min(rows, FFN_SUB)
            for u in range(FFN_SUB // piece):
                lo = m * FFN_SUB + u * piece
                o_ref[lo:lo + piece, :] += gate_ref[lo // rows] * res[u * piece:(u + 1) * piece, :]

    _by_parity(i, h_a, h_b, body)

    if final_norm:
        @pl.when(f == pl.num_programs(1) - 1)
        def _():
            def body(r, carry):
                rs = pl.ds(pl.multiple_of(r * FILL_ROWS, FILL_ROWS), FILL_ROWS)
                y = o_ref[rs, :]
                o_ref[rs, :] = y * lax.rsqrt(jnp.mean(y * y, axis=-1, keepdims=True) + EPS) * fin_ref[...]
                return carry

            lax.fori_loop(0, tm // FILL_ROWS, body, 0)


def _ffn(x2, seq_len, gain, mod, row0, w_in, w_out, layer, fin_gain, final_norm, tm=1024, tf=512):
    rows = x2.shape[0]
    nb, tps = _row_tiling(seq_len, tm)
    nf = D_FF // tf
    n_tiles = rows // tm
    assert nf - 1 >= tm // FILL_ROWS
    vec_spec = pl.BlockSpec((1, D_MODEL), lambda i, f: (0, 0))
    x_tile = lambda i, f: jnp.minimum(i + jnp.minimum(f, 1), n_tiles - 1)
    return pl.pallas_call(
        functools.partial(_ffn_kernel, nb=nb, final_norm=final_norm),
        grid=(n_tiles, nf),
        in_specs=[
            pl.BlockSpec((tm, D_MODEL), lambda i, f: (x_tile(i, f), 0), pipeline_mode=pl.Buffered(1)),
            vec_spec,
            _mod_spec(SCALE2, row0, nb, tps, tile=x_tile), _mod_spec(SHIFT2, row0, nb, tps, tile=x_tile),
            _mod_spec(GATE2, row0, nb, tps),
            pl.BlockSpec((None, D_MODEL, tf), lambda i, f: (layer, 0, f)),
            pl.BlockSpec((None, D_MODEL, tf), lambda i, f: (layer, 0, nf + f)),
            pl.BlockSpec((None, tf, D_MODEL), lambda i, f: (layer, f, 0)),
            vec_spec,
        ],
        out_specs=pl.BlockSpec((tm, D_MODEL), lambda i, f: (i, 0)),
        out_shape=jax.ShapeDtypeStruct((rows, D_MODEL), F32),
        scratch_shapes=[pltpu.VMEM((tm, D_MODEL), BF16), pltpu.VMEM((tm, D_MODEL), BF16)],
        compiler_params=_params(("arbitrary", "arbitrary")),
        name="ffn",
    )(x2, gain, mod, mod, mod, w_in, w_in, w_out, fin_gain)


def _qkv_kernel(x_ref, gain_ref, scale_ref, shift_ref, wq_ref, wkv_ref, q_ref, kv_ref, h_scr, *, nb):
    @pl.when(pl.program_id(1) == 0)
    def _():
        _fill_h(h_scr, x_ref, gain_ref, scale_ref, shift_ref, nb)
        kv_ref[...] = jnp.dot(h_scr[...], wkv_ref[...], preferred_element_type=F32)

    q = jnp.dot(h_scr[...], wq_ref[...], preferred_element_type=F32)
    q_ref[...] = (q * (HEAD_DIM ** -0.5)).astype(BF16)


def _qkv(x2, seq_len, gain, mod, row0, w_qkv, tm=1024, tn=512):
    rows = x2.shape[0]
    nb, tps = _row_tiling(seq_len, tm)
    nq = (N_HEADS * HEAD_DIM) // tn
    kv_cols = 2 * N_KV * HEAD_DIM
    assert kv_cols == tn
    return pl.pallas_call(
        functools.partial(_qkv_kernel, nb=nb),
        grid=(rows // tm, nq),
        in_specs=[
            pl.BlockSpec((tm, D_MODEL), lambda i, j: (i, 0)),
            pl.BlockSpec((1, D_MODEL), lambda i, j: (0, 0)),
            _mod_spec(SCALE1, row0, nb, tps), _mod_spec(SHIFT1, row0, nb, tps),
            pl.BlockSpec((D_MODEL, tn), lambda i, j: (0, j)),
            pl.BlockSpec((D_MODEL, kv_cols), lambda i, j: (0, nq)),
        ],
        out_specs=[
            pl.BlockSpec((tm, tn), lambda i, j: (i, j)),
            pl.BlockSpec((tm, kv_cols), lambda i, j: (i, 0)),
        ],
        out_shape=[
            jax.ShapeDtypeStruct((rows, N_HEADS * HEAD_DIM), BF16),
            jax.ShapeDtypeStruct((rows, kv_cols), F32),
        ],
        scratch_shapes=[pltpu.VMEM((tm, D_MODEL), BF16)],
        compiler_params=_params(("arbitrary", "arbitrary")),
        name="qkv",
    )(x2, gain, mod, mod, w_qkv, w_qkv)


PAIRS_PER_KV = GROUP // 2
QROWS = PAIRS_PER_KV * CHUNK
BAND = WINDOW + CHUNK
ATT_G = 8
KVW = N_KV * HEAD_DIM


def _attn_kernel(slope_ref, sink_ref, q_ref, *refs, band, steps_per_seq):
    if band:
        kp2_ref, kp1_ref, kc_ref, vp2_ref, vp1_ref, vc_ref = refs[:6]
        refs = refs[6:]
    else:
        ck_ref, kn_ref, cv_ref, vn_ref = refs[:4]
        refs = refs[4:]
    o_ref, bias_scr, ones_scr, kl_scr, kr_scr, vl_scr, vr_scr = refs
    step = pl.program_id(0)

    @pl.when(step == 0)
    def _():
        row = lax.broadcasted_iota(jnp.int32, (QROWS, 2 * KPAD), 0)
        col = lax.broadcasted_iota(jnp.int32, (QROWS, 2 * KPAD), 1)
        key = jnp.bitwise_and(col, KPAD - 1)
        dist = jnp.abs(WINDOW + jnp.bitwise_and(row, CHUNK - 1) - key).astype(F32)
        for kv in range(N_KV):
            slope = jnp.zeros((QROWS, 2 * KPAD), F32)
            sink = jnp.zeros((QROWS, 2 * KPAD), F32)
            for pr in range(PAIRS_PER_KV):
                in_pair = jnp.logical_and(row >= pr * CHUNK, row < (pr + 1) * CHUNK)
                in_a = jnp.logical_and(in_pair, col < KPAD)
                in_b = jnp.logical_and(in_pair, col >= KPAD)
                head = kv * GROUP + pr * 2
                slope = jnp.where(in_a, slope_ref[head], jnp.where(in_b, slope_ref[head + 1], slope))
                sink = jnp.where(in_a, sink_ref[head], jnp.where(in_b, sink_ref[head + 1], sink))
            bias_scr[kv] = jnp.where(key < BAND, -(slope * dist), jnp.where(key == BAND, sink, NEG))
        orow = lax.broadcasted_iota(jnp.int32, (2 * KPAD, PAIR), 0)
        ocol = lax.broadcasted_iota(jnp.int32, (2 * KPAD, PAIR), 1)
        in_first = ocol < HEAD_DIM
        ones_scr[...] = jnp.where(orow < KPAD, jnp.where(in_first, 1.0, 0.0),
                                  jnp.where(in_first, 0.0, 1.0)).astype(BF16)

    lane64 = lax.broadcasted_iota(jnp.int32, (CHUNK, PAIR), 1) < HEAD_DIM

    def prep(src_ref, r_src, col0, r_dst, l_scr, r_scr):
        for lb in range(N_KV // 2):
            blk = src_ref[r_src:r_src + CHUNK, col0 + lb * PAIR:col0 + (lb + 1) * PAIR]
            ev = jnp.where(lane64, blk, 0.0)
            od = jnp.where(lane64, 0.0, blk)
            ds = slice(r_dst, r_dst + CHUNK)
            l_scr[2 * lb, ds, :] = ev.astype(BF16)
            r_scr[2 * lb, ds, :] = pltpu.roll(ev, HEAD_DIM, axis=1).astype(BF16)
            r_scr[2 * lb + 1, ds, :] = od.astype(BF16)
            l_scr[2 * lb + 1, ds, :] = pltpu.roll(od, HEAD_DIM, axis=1).astype(BF16)

    if band:
        prep(kp2_ref, 0, 0, 0, kl_scr, kr_scr)
        prep(kp1_ref, 0, 0, CHUNK, kl_scr, kr_scr)
        prep(vp2_ref, 0, 0, 0, vl_scr, vr_scr)
        prep(vp1_ref, 0, 0, CHUNK, vl_scr, vr_scr)
        for g in range(ATT_G):
            prep(kc_ref, g * CHUNK, 0, WINDOW + g * CHUNK, kl_scr, kr_scr)
            prep(vc_ref, g * CHUNK, 0, WINDOW + g * CHUNK, vl_scr, vr_scr)
    else:
        for g in range(ATT_G):
            for h in range(WINDOW // CHUNK):
                prep(ck_ref, g * WINDOW + h * CHUNK, 0, g * BAND + h * CHUNK, kl_scr, kr_scr)
                prep(cv_ref, g * WINDOW + h * CHUNK, 0, g * BAND + h * CHUNK, vl_scr, vr_scr)
            prep(kn_ref, g * CHUNK, 0, g * BAND + WINDOW, kl_scr, kr_scr)
            prep(vn_ref, g * CHUNK, 0, g * BAND + WINDOW, vl_scr, vr_scr)

    zpad = jnp.zeros((KPAD - BAND, PAIR), BF16)
    keyrow = jnp.bitwise_and(lax.broadcasted_iota(jnp.int32, (1, 2 * KPAD), 1), KPAD - 1)
    n0 = (step % steps_per_seq) * ATT_G

    def chunk(c, carry, masked):
        r0 = pl.multiple_of(c * CHUNK, CHUNK)
        k0 = r0 if band else pl.multiple_of(c * BAND, CHUNK)
        ks = pl.ds(k0, BAND)
        qrows = pl.ds(r0, CHUNK)
        if masked:
            maskrow = jnp.where(keyrow >= WINDOW - (n0 + c) * CHUNK, 0.0, NEG)
        for kv in range(N_KV):
            qs = jnp.concatenate(
                [q_ref[qrows, kv * GROUP * HEAD_DIM + pr * PAIR: kv * GROUP * HEAD_DIM + (pr + 1) * PAIR]
                 for pr in range(PAIRS_PER_KV)], axis=0)
            kbd = jnp.concatenate([kl_scr[kv, ks, :], zpad, kr_scr[kv, ks, :], zpad], axis=0)
            vbd = jnp.concatenate([vl_scr[kv, ks, :], zpad, vr_scr[kv, ks, :], zpad], axis=0)
            s = lax.dot_general(qs, kbd, (((1,), (1,)), ((), ())), preferred_element_type=F32)
            s = s + bias_scr[kv]
            if masked:
                s = s + maskrow
            s_a = s[:, :KPAD]
            s_b = s[:, KPAD:]
            m_a = jnp.max(s_a, axis=-1, keepdims=True)
            m_b = jnp.max(s_b, axis=-1, keepdims=True)
            p = jnp.concatenate([jnp.exp(s_a - m_a), jnp.exp(s_b - m_b)], axis=1).astype(BF16)
            ol = jnp.dot(p, jnp.concatenate([vbd, ones_scr[...]], axis=1), preferred_element_type=F32)
            o = ol[:, :PAIR] / ol[:, PAIR:]
            for pr in range(PAIRS_PER_KV):
                c0 = kv * GROUP * HEAD_DIM + pr * PAIR
                o_ref[qrows, c0:c0 + PAIR] = o[pr * CHUNK:(pr + 1) * CHUNK, :].astype(BF16)
        return carry

    n_lead = WINDOW // CHUNK if band else 0
    if n_lead:
        lax.fori_loop(0, n_lead, functools.partial(chunk, masked=True), 0, unroll=2)
    lax.fori_loop(n_lead, ATT_G, functools.partial(chunk, masked=False), 0, unroll=2)


def _attention(q, kv, cache_k, cache_v, slopes, sinks, seq_len):
    rows = q.shape[0]
    tq = ATT_G * CHUNK
    band = cache_k is None
    smem = pl.BlockSpec(memory_space=pltpu.SMEM)
    if band:
        assert seq_len % tq == 0
        sps = seq_len // tq
        cps = seq_len // CHUNK

        def prev(d, col):
            return pl.BlockSpec((CHUNK, KVW), lambda i: (jnp.maximum((i % sps) * ATT_G - d, 0) + (i // sps) * cps, col))

        kv_specs = [prev(2, 0), prev(1, 0), pl.BlockSpec((tq, KVW), lambda i: (i, 0)),
                    prev(2, 1), prev(1, 1), pl.BlockSpec((tq, KVW), lambda i: (i, 1))]
        kv_args = [kv] * 6
        key_rows = WINDOW + tq
    else:
        assert seq_len == CHUNK
        sps = 1
        kv_specs = [pl.BlockSpec((ATT_G * WINDOW, KVW), lambda i: (i, 0)), pl.BlockSpec((tq, KVW), lambda i: (i, 0)),
                    pl.BlockSpec((ATT_G * WINDOW, KVW), lambda i: (i, 0)), pl.BlockSpec((tq, KVW), lambda i: (i, 1))]
        kv_args = [cache_k, kv, cache_v, kv]
        key_rows = ATT_G * BAND
    side = pltpu.VMEM((N_KV, key_rows, PAIR), BF16)
    return pl.pallas_call(
        functools.partial(_attn_kernel, band=band, steps_per_seq=sps),
        grid=(rows // tq,),
        in_specs=[smem, smem, pl.BlockSpec((tq, N_HEADS * HEAD_DIM), lambda i: (i, 0))] + kv_specs,
        out_specs=pl.BlockSpec((tq, N_HEADS * HEAD_DIM), lambda i: (i, 0)),
        out_shape=jax.ShapeDtypeStruct((rows, N_HEADS * HEAD_DIM), BF16),
        scratch_shapes=[
            pltpu.VMEM((N_KV, QROWS, 2 * KPAD), F32),
            pltpu.VMEM((2 * KPAD, PAIR), BF16),
            side, side, side, side,
        ],
        compiler_params=_params(("arbitrary",)),
        name="attention",
    )(slopes, sinks, q, *kv_args)


def _wo_kernel(a_ref, w_ref, x_ref, gate_ref, o_ref, *, nb):
    out = jnp.dot(a_ref[...], w_ref[...], preferred_element_type=F32)
    rows = a_ref.shape[0] // nb
    for s in range(nb):
        sl = slice(s * rows, (s + 1) * rows)
        o_ref[sl, :] = x_ref[sl, :] + gate_ref[s] * out[sl, :]


def _wo(attn, x2, seq_len, mod, row0, w_o, tm=1024, tn=1024):
    rows = x2.shape[0]
    nb, tps = _row_tiling(seq_len, tm)
    return pl.pallas_call(
        functools.partial(_wo_kernel, nb=nb),
        grid=(rows // tm, D_MODEL // tn),
        in_specs=[
            pl.BlockSpec((tm, N_HEADS * HEAD_DIM), lambda i, j: (i, 0)),
            pl.BlockSpec((N_HEADS * HEAD_DIM, tn), lambda i, j: (0, j)),
            pl.BlockSpec((tm, tn), lambda i, j: (i, j)),
            _mod_spec(GATE1, row0, nb, tps, width=tn),
        ],
        out_specs=pl.BlockSpec((tm, tn), lambda i, j: (i, j)),
        out_shape=jax.ShapeDtypeStruct((rows, D_MODEL), F32),
        compiler_params=_params(("arbitrary", "arbitrary")),
        name="wo",
    )(attn, w_o, x2, mod)


def _trunk(x, mods, row0, state_conv, cache_k, cache_v, wts, late):
    batch, seq_len, _ = x.shape
    rows = batch * seq_len
    x2 = x.reshape(rows, D_MODEL)
    prompt = state_conv is None
    row = lambda v: v.reshape(1, -1)
    assert seq_len >= CONV_W - 1 and seq_len >= WINDOW // 2

    g = _pw1_glu(x2, seq_len, row(wts["norm_mix_g"][0]), mods[0], row0, wts["w_pw1"], row(wts["b_pw1"][0]))
    if prompt:
        prev = jnp.zeros((batch, HALO, D_MODEL), F32)
    else:
        prev = jnp.pad(state_conv[0], ((0, 0), (HALO - (CONV_W - 1), 0), (0, 0)))
    conv_new = g.reshape(batch, seq_len, D_MODEL)[:, -(CONV_W - 1):][None]
    casts = ()
    if late is None:
        casts = (wts["w_ffn_in"].reshape(DEPTH * D_MODEL, 2 * D_FF), wts["w_ffn_out"].reshape(DEPTH * D_FF, D_MODEL),
                 wts["w_qkv"][0], wts["w_o"][0])
    x2, cast_out = _conv_pw2(g, prev, x2, seq_len, mods[0], row0, wts["w_dw"][0], row(wts["b_dw"][0]),
                             row(wts["ln_g"][0]), row(wts["ln_b"][0]), wts["w_pw2"], row(wts["b_pw2"][0]), casts)
    if late is None:
        late = dict(w_ffn_in=cast_out[0].reshape(DEPTH, D_MODEL, 2 * D_FF),
                    w_ffn_out=cast_out[1].reshape(DEPTH, D_FF, D_MODEL), w_qkv=cast_out[2], w_o=cast_out[3])
    x2 = _ffn(x2, seq_len, row(wts["norm_ffn_g"][0]), mods[0], row0, late["w_ffn_in"], late["w_ffn_out"], 0,
              row(wts["final_norm_g"]), final_norm=False)

    q, kv = _qkv(x2, seq_len, row(wts["norm_mix_g"][1]), mods[1], row0, late["w_qkv"])
    kv3 = kv.reshape(batch, seq_len, 2 * KVW)
    if prompt:
        attn = _attention(q, kv, None, None, wts["slopes"], wts["sinks"], seq_len)
        k_new = kv3[:, -WINDOW:, :KVW]
        v_new = kv3[:, -WINDOW:, KVW:]
    else:
        ck = cache_k[0].reshape(batch * WINDOW, KVW)
        cv = cache_v[0].reshape(batch * WINDOW, KVW)
        attn = _attention(q, kv, ck, cv, wts["slopes"], wts["sinks"], seq_len)
        keep = WINDOW - seq_len
        k_new = jnp.concatenate([cache_k[0].reshape(batch, WINDOW, KVW)[:, -keep:], kv3[:, :, :KVW]], axis=1)
        v_new = jnp.concatenate([cache_v[0].reshape(batch, WINDOW, KVW)[:, -keep:], kv3[:, :, KVW:]], axis=1)
    x2 = _wo(attn, x2, seq_len, mods[1], row0, late["w_o"])
    y2 = _ffn(x2, seq_len, row(wts["norm_ffn_g"][1]), mods[1], row0, late["w_ffn_in"], late["w_ffn_out"], 1,
              row(wts["final_norm_g"]), final_norm=True)

    shape_kv = (1, batch, WINDOW, N_KV, HEAD_DIM)
    return (y2.reshape(batch, seq_len, D_MODEL), conv_new,
            k_new.reshape(shape_kv), v_new.reshape(shape_kv)), late


def kernel(x_prompt, x_sample, c_prompt, c_sample, state_conv, cache_k, cache_v, norm_mix_g, norm_ffn_g,
           w_ada, b_ada, w_pw1, b_pw1, w_dw, b_dw, ln_g, ln_b, w_pw2, b_pw2, w_qkv, w_o, sinks,
           w_ffn_in, w_ffn_out, final_norm_g):
    bp, bs = c_prompt.shape[0], c_sample.shape[0]
    b_pad = -(-(bs + bp) // 8) * 8
    c_all = jnp.concatenate([c_sample, c_prompt, jnp.zeros((b_pad - bp - bs, D_MODEL), F32)], axis=0)
    mod = _adaln(c_all, w_ada, b_ada)
    mods = [mod[l].reshape(b_pad, 1, 6 * D_MODEL) for l in range(DEPTH)]

    heads = jnp.arange(1, N_HEADS + 1, dtype=F32)
    wts = dict(
        norm_mix_g=norm_mix_g, norm_ffn_g=norm_ffn_g, final_norm_g=final_norm_g,
        w_pw1=w_pw1[0].astype(BF16), b_pw1=b_pw1, w_dw=w_dw, b_dw=b_dw, ln_g=ln_g, ln_b=ln_b,
        w_pw2=w_pw2[0].astype(BF16), b_pw2=b_pw2, w_qkv=w_qkv, w_o=w_o, w_ffn_in=w_ffn_in, w_ffn_out=w_ffn_out,
        slopes=jnp.exp2(-8.0 * heads / N_HEADS), sinks=sinks[0].astype(F32),
    )
    (y_p, conv_p, k_p, v_p), late = _trunk(x_prompt, mods, bs, None, None, None, wts, None)
    (y_s, conv_s, k_s, v_s), _ = _trunk(x_sample, mods, 0, state_conv, cache_k, cache_v, wts, late)
    return (y_p, y_s, conv_p, conv_s, k_p, v_p, k_s, v_s)
```

```python
import functools

import jax
import jax.numpy as jnp
from jax import lax
from jax.experimental import pallas as pl
from jax.experimental.pallas import tpu as pltpu

F32 = jnp.float32
BF16 = jnp.bfloat16

D_MODEL = 2048
DEPTH = 2
CHUNK = 64
CONV_W = 31
N_HEADS = 32
N_KV = 4
HEAD_DIM = 64
GROUP = N_HEADS // N_KV
WINDOW = 128
D_FF = 5632
EPS = 1e-6
NEG = -1e30

VMEM_LIMIT_BYTES = 60 * 1024 * 1024
LANES = 128
HALO = 32
PAIR = 2 * HEAD_DIM
KPAD = 256

SHIFT1, SCALE1, GATE1, SHIFT2, SCALE2, GATE2 = range(6)


def _params(sem):
    return pltpu.CompilerParams(dimension_semantics=sem, vmem_limit_bytes=VMEM_LIMIT_BYTES)


def _silu(v):
    return v * jax.nn.sigmoid(v)


FILL_ROWS = 128


def _fill_rows(h_ref, x_ref, gain_ref, scale_ref, shift_ref, nb, start, n_rows):
    rows = x_ref.shape[0] // nb
    blk = min(rows, FILL_ROWS)
    for u in range(n_rows // blk):
        at = start + u * blk
        s = 0 if nb == 1 else at // rows
        rs = pl.ds(pl.multiple_of(at, blk), blk)
        x = x_ref[rs, :]
        y = x * lax.rsqrt(jnp.mean(x * x, axis=-1, keepdims=True) + EPS)
        mul = gain_ref[...] * (1.0 + scale_ref[s])
        h_ref[rs, :] = (y * mul + shift_ref[s]).astype(BF16)


def _fill_h(h_ref, x_ref, gain_ref, scale_ref, shift_ref, nb):
    def body(r, carry):
        _fill_rows(h_ref, x_ref, gain_ref, scale_ref, shift_ref, nb, r * FILL_ROWS, FILL_ROWS)
        return carry

    lax.fori_loop(0, x_ref.shape[0] // FILL_ROWS, body, 0)


def _by_parity(i, h_a, h_b, body):
    @pl.when(i % 2 == 0)
    def _():
        body(h_a, h_b)

    @pl.when(i % 2 == 1)
    def _():
        body(h_b, h_a)


def _adaln_kernel(c_ref, w_ref, b_ref, o_ref):
    c = c_ref[...]
    a = _silu(c).astype(BF16)
    o_ref[0] = jnp.dot(a, w_ref[0].astype(BF16), preferred_element_type=F32) + b_ref[0]


def _adaln(c_all, w_ada, b_ada, tn=1024):
    bp = c_all.shape[0]
    n = w_ada.shape[-1]
    return pl.pallas_call(
        _adaln_kernel,
        grid=(DEPTH, n // tn),
        in_specs=[
            pl.BlockSpec((bp, D_MODEL), lambda l, j: (0, 0)),
            pl.BlockSpec((1, D_MODEL, tn), lambda l, j: (l, 0, j)),
            pl.BlockSpec((1, 1, tn), lambda l, j: (l, 0, j)),
        ],
        out_specs=pl.BlockSpec((1, bp, tn), lambda l, j: (l, 0, j)),
        out_shape=jax.ShapeDtypeStruct((DEPTH, bp, n), F32),
        compiler_params=_params(("arbitrary", "arbitrary")),
        name="adaln",
    )(c_all, w_ada, b_ada.reshape(DEPTH, 1, n))


def _row_tiling(seq_len, tm):
    if tm >= seq_len:
        assert tm % seq_len == 0
        return tm // seq_len, 1
    assert seq_len % tm == 0
    return 1, seq_len // tm


def _mod_spec(which, row0, nb, tps, width=D_MODEL, tile=lambda i, j: i):
    assert row0 % nb == 0
    per = D_MODEL // width
    if per == 1:
        return pl.BlockSpec((nb, 1, width), lambda i, j: (row0 // nb + tile(i, j) // tps, 0, which))
    return pl.BlockSpec((nb, 1, width), lambda i, j: (row0 // nb + tile(i, j) // tps, 0, which * per + j))


def _pw1_glu_kernel(x_ref, gain_ref, scale_ref, shift_ref, wa_ref, wb_ref, ba_ref, bb_ref,
                    g_ref, h_scr, *, nb):
    @pl.when(pl.program_id(1) == 0)
    def _():
        _fill_h(h_scr, x_ref, gain_ref, scale_ref, shift_ref, nb)

    h = h_scr[...]
    a = jnp.dot(h, wa_ref[...], preferred_element_type=F32) + ba_ref[...]
    b = jnp.dot(h, wb_ref[...], preferred_element_type=F32) + bb_ref[...]
    g_ref[...] = a * jax.nn.sigmoid(b)


def _pw1_glu(x2, seq_len, gain, mod, row0, w1, b1, tm=1024, tn=512):
    rows = x2.shape[0]
    nb, tps = _row_tiling(seq_len, tm)
    nj = D_MODEL // tn
    return pl.pallas_call(
        functools.partial(_pw1_glu_kernel, nb=nb),
        grid=(rows // tm, nj),
        in_specs=[
            pl.BlockSpec((tm, D_MODEL), lambda i, j: (i, 0)),
            pl.BlockSpec((1, D_MODEL), lambda i, j: (0, 0)),
            _mod_spec(SCALE1, row0, nb, tps), _mod_spec(SHIFT1, row0, nb, tps),
            pl.BlockSpec((D_MODEL, tn), lambda i, j: (0, j)),
            pl.BlockSpec((D_MODEL, tn), lambda i, j: (0, nj + j)),
            pl.BlockSpec((1, tn), lambda i, j: (0, j)),
            pl.BlockSpec((1, tn), lambda i, j: (0, nj + j)),
        ],
        out_specs=pl.BlockSpec((tm, tn), lambda i, j: (i, j)),
        out_shape=jax.ShapeDtypeStruct((rows, D_MODEL), F32),
        scratch_shapes=[pltpu.VMEM((tm, D_MODEL), BF16)],
        compiler_params=_params(("arbitrary", "arbitrary")),
        name="pw1_glu",
    )(x2, gain, mod, mod, w1, w1, b1, b1)


CONV_ROWS = 64
LN_ROWS = 128
NCHUNK = D_MODEL // LANES


def _conv_pw2_kernel(g_ref, halo_ref, prev_ref, x_ref, gate_ref, wdw_ref, bdw_ref, lng_ref, lnb_ref,
                     w2_ref, b2_ref, *rest, nb, tps, n_cast):
    cast_src = rest[:n_cast]
    o_ref = rest[n_cast]
    cast_dst = rest[n_cast + 1:2 * n_cast + 1]
    win_scr, y_scr, ybf_scr = rest[2 * n_cast + 1:]
    i = pl.program_id(0)
    j = pl.program_id(1)
    nj = pl.num_programs(1)
    tm = g_ref.shape[0]
    tt = tm // nb
    ti = jnp.minimum(i, pl.num_programs(0) - 2)
    cur = i % 2
    chunks_per_step = w2_ref.shape[1] // LANES

    for src, dst in zip(cast_src, cast_dst):
        dst[...] = src[...].astype(BF16)

    drain = i == pl.num_programs(0) - 1

    @pl.when(jnp.logical_and(i == 0, j == 0))
    def _():
        ybf_scr[1] = jnp.zeros(ybf_scr.shape[1:], BF16)

    @pl.when(jnp.logical_and(j == 0, jnp.logical_not(drain)))
    def _():
        def put_history(src_of):
            for s in range(nb):
                for c in range(NCHUNK):
                    win_scr[s * NCHUNK + c, 0:HALO, :] = src_of(s, c)

        if tps == 1:
            put_history(lambda s, c: prev_ref[s, :, c * LANES:(c + 1) * LANES])
        else:
            first = (ti % tps) == 0

            @pl.when(first)
            def _():
                put_history(lambda s, c: prev_ref[s, :, c * LANES:(c + 1) * LANES])

            @pl.when(jnp.logical_not(first))
            def _():
                put_history(lambda s, c: halo_ref[:, c * LANES:(c + 1) * LANES])
        for s in range(nb):
            for c in range(NCHUNK):
                win_scr[s * NCHUNK + c, HALO:HALO + tt, :] = g_ref[s * tt:(s + 1) * tt, c * LANES:(c + 1) * LANES]

    lead = HALO - (CONV_W - 1)

    def matmul_prev():
        out = jnp.dot(ybf_scr[1 - cur], w2_ref[...], preferred_element_type=F32) + b2_ref[...]
        for s in range(nb):
            sl = slice(s * tt, (s + 1) * tt)
            o_ref[sl, :] = x_ref[sl, :] + gate_ref[s] * out[sl, :]

    def conv_and_matmul(jj):
        for cc in range(chunks_per_step):
            c = jj * chunks_per_step + cc
            for s in range(nb):
                for rb in range(tt // CONV_ROWS):
                    acc = jnp.zeros((CONV_ROWS, LANES), F32) + bdw_ref[c]
                    for k in range(CONV_W):
                        r0 = rb * CONV_ROWS + k + lead
                        acc = acc + win_scr[s * NCHUNK + c, r0:r0 + CONV_ROWS, :] * wdw_ref[c, k:k + 1, :]
                    y_scr[c, s * tt + rb * CONV_ROWS:s * tt + (rb + 1) * CONV_ROWS, :] = acc
        matmul_prev()

    for jj in range(NCHUNK // chunks_per_step):
        pl.when(jnp.logical_and(j == jj, jnp.logical_not(drain)))(functools.partial(conv_and_matmul, jj))
    pl.when(drain)(matmul_prev)

    @pl.when(jnp.logical_and(j == nj - 1, jnp.logical_not(drain)))
    def _():
        def ln_block(r, carry):
            rs = pl.ds(pl.multiple_of(r * LN_ROWS, LN_ROWS), LN_ROWS)
            tot = y_scr[0, rs, :]
            for c in range(1, NCHUNK):
                tot = tot + y_scr[c, rs, :]
            mu = jnp.sum(tot, axis=-1, keepdims=True) * (1.0 / D_MODEL)
            sq = jnp.zeros((LN_ROWS, LANES), F32)
            for c in range(NCHUNK):
                yc = y_scr[c, rs, :] - mu
                sq = sq + yc * yc
            rstd = lax.rsqrt(jnp.sum(sq, axis=-1, keepdims=True) * (1.0 / D_MODEL) + EPS)
            for c in range(NCHUNK):
                cs = slice(c * LANES, (c + 1) * LANES)
                z = (y_scr[c, rs, :] - mu) * rstd * lng_ref[:, cs] + lnb_ref[:, cs]
                ybf_scr[cur, rs, cs] = _silu(z).astype(BF16)
            return carry

        lax.fori_loop(0, tm // LN_ROWS, ln_block, 0)


def _conv_pw2(g2, prev, x2, seq_len, mod, row0, w_dw, b_dw, ln_g, ln_b, w2, b2, casts=(), tm=512, tn=512):
    rows = g2.shape[0]
    nb, tps = _row_tiling(seq_len, tm)
    tt = tm // nb
    hb = tm // HALO
    nj = D_MODEL // tn
    n_tiles = rows // tm
    n_steps = n_tiles * nj
    cast_specs = []
    for a in casts:
        slab = a.shape[0] // n_steps
        assert a.shape[0] % n_steps == 0 and slab % 16 == 0, a.shape
        cast_specs.append(pl.BlockSpec((slab, a.shape[1]), lambda i, j: (jnp.minimum(i * nj + j, n_steps - 1), 0)))
    front = lambda i: jnp.minimum(i, n_tiles - 1)
    back = lambda i: jnp.maximum(i - 1, 0)
    assert row0 % nb == 0
    outs = pl.pallas_call(
        functools.partial(_conv_pw2_kernel, nb=nb, tps=tps, n_cast=len(casts)),
        grid=(n_tiles + 1, nj),
        in_specs=[
            pl.BlockSpec((tm, D_MODEL), lambda i, j: (front(i), 0)),
            pl.BlockSpec((HALO, D_MODEL), lambda i, j: (jnp.maximum(front(i) * hb - 1, 0), 0)),
            pl.BlockSpec((nb, HALO, D_MODEL), lambda i, j: (front(i) // tps, 0, 0)),
            pl.BlockSpec((tm, tn), lambda i, j: (back(i), j)),
            pl.BlockSpec((nb, 1, tn), lambda i, j: (row0 // nb + back(i) // tps, 0, GATE1 * nj + j)),
            pl.BlockSpec((NCHUNK, CONV_W, LANES), lambda i, j: (0, 0, 0)),
            pl.BlockSpec((NCHUNK, 1, LANES), lambda i, j: (0, 0, 0)),
            pl.BlockSpec((1, D_MODEL), lambda i, j: (0, 0)),
            pl.BlockSpec((1, D_MODEL), lambda i, j: (0, 0)),
            pl.BlockSpec((D_MODEL, tn), lambda i, j: (0, j)),
            pl.BlockSpec((1, tn), lambda i, j: (0, j)),
        ] + cast_specs,
        out_specs=[pl.BlockSpec((tm, tn), lambda i, j: (back(i), j * jnp.minimum(i, 1)))] + cast_specs,
        out_shape=[jax.ShapeDtypeStruct((rows, D_MODEL), F32)]
        + [jax.ShapeDtypeStruct(a.shape, BF16) for a in casts],
        scratch_shapes=[
            pltpu.VMEM((nb * NCHUNK, HALO + tt, LANES), F32),
            pltpu.VMEM((NCHUNK, tm, LANES), F32),
            pltpu.VMEM((2, tm, D_MODEL), BF16),
        ],
        compiler_params=_params(("arbitrary", "arbitrary")),
        name="conv_pw2",
    )(g2, g2, prev, x2, mod,
      w_dw.reshape(CONV_W, NCHUNK, LANES).transpose(1, 0, 2), b_dw.reshape(NCHUNK, 1, LANES),
      ln_g, ln_b, w2, b2, *casts)
    return outs[0], list(outs[1:])


FFN_SUB = 512


def _ffn_kernel(x_ref, gain_ref, scale_ref, shift_ref, gate_ref, wa_ref, wb_ref, wo_ref, fin_ref,
                o_ref, h_a, h_b, *, nb, final_norm):
    i = pl.program_id(0)
    f = pl.program_id(1)
    tm = x_ref.shape[0]
    rows = tm // nb

    @pl.when(f == 0)
    def _():
        @pl.when(i == 0)
        def _():
            _fill_h(h_a, x_ref, gain_ref, scale_ref, shift_ref, nb)

        o_ref[...] = x_ref[...]

    slab = jnp.clip(f - 1, 0, tm // FILL_ROWS - 1)

    def body(h_cur, h_next):
        _fill_rows(h_next, x_ref, gain_ref, scale_ref, shift_ref, nb, slab * FILL_ROWS, FILL_ROWS)
        for m in range(tm // FFN_SUB):
            h = h_cur[m * FFN_SUB:(m + 1) * FFN_SUB, :]
            a = jnp.dot(h, wa_ref[...], preferred_element_type=F32)
            b = jnp.dot(h, wb_ref[...], preferred_element_type=F32)
            act = (_silu(a) * b).astype(BF16)
            res = jnp.dot(act, wo_ref[...], preferred_element_type=F32)
            piece = min(rows, FFN_SUB)
            for u in range(FFN_SUB // piece):
                lo = m * FFN_SUB + u * piece
                o_ref[lo:lo + piece, :] += gate_ref[lo // rows] * res[u * piece:(u + 1) * piece, :]

    _by_parity(i, h_a, h_b, body)

    if final_norm:
        @pl.when(f == pl.num_programs(1) - 1)
        def _():
            def body(r, carry):
                rs = pl.ds(pl.multiple_of(r * FILL_ROWS, FILL_ROWS), FILL_ROWS)
                y = o_ref[rs, :]
                o_ref[rs, :] = y * lax.rsqrt(jnp.mean(y * y, axis=-1, keepdims=True) + EPS) * fin_ref[...]
                return carry

            lax.fori_loop(0, tm // FILL_ROWS, body, 0)


def _ffn(x2, seq_len, gain, mod, row0, w_in, w_out, layer, fin_gain, final_norm, tm=1024, tf=512):
    rows = x2.shape[0]
    nb, tps = _row_tiling(seq_len, tm)
    nf = D_FF // tf
    n_tiles = rows // tm
    assert nf - 1 >= tm // FILL_ROWS
    vec_spec = pl.BlockSpec((1, D_MODEL), lambda i, f: (0, 0))
    x_tile = lambda i, f: jnp.minimum(i + jnp.minimum(f, 1), n_tiles - 1)
    return pl.pallas_call(
        functools.partial(_ffn_kernel, nb=nb, final_norm=final_norm),
        grid=(n_tiles, nf),
        in_specs=[
            pl.BlockSpec((tm, D_MODEL), lambda i, f: (x_tile(i, f), 0), pipeline_mode=pl.Buffered(1)),
            vec_spec,
            _mod_spec(SCALE2, row0, nb, tps, tile=x_tile), _mod_spec(SHIFT2, row0, nb, tps, tile=x_tile),
            _mod_spec(GATE2, row0, nb, tps),
            pl.BlockSpec((None, D_MODEL, tf), lambda i, f: (layer, 0, f)),
            pl.BlockSpec((None, D_MODEL, tf), lambda i, f: (layer, 0, nf + f)),
            pl.BlockSpec((None, tf, D_MODEL), lambda i, f: (layer, f, 0)),
            vec_spec,
        ],
        out_specs=pl.BlockSpec((tm, D_MODEL), lambda i, f: (i, 0)),
        out_shape=jax.ShapeDtypeStruct((rows, D_MODEL), F32),
        scratch_shapes=[pltpu.VMEM((tm, D_MODEL), BF16), pltpu.VMEM((tm, D_MODEL), BF16)],
        compiler_params=_params(("arbitrary", "arbitrary")),
        name="ffn",
    )(x2, gain, mod, mod, mod, w_in, w_in, w_out, fin_gain)


def _qkv_kernel(x_ref, gain_ref, scale_ref, shift_ref, wq_ref, wkv_ref, q_ref, kv_ref, h_scr, *, nb):
    @pl.when(pl.program_id(1) == 0)
    def _():
        _fill_h(h_scr, x_ref, gain_ref, scale_ref, shift_ref, nb)
        kv_ref[...] = jnp.dot(h_scr[...], wkv_ref[...], preferred_element_type=F32)

    q = jnp.dot(h_scr[...], wq_ref[...], preferred_element_type=F32)
    q_ref[...] = (q * (HEAD_DIM ** -0.5)).astype(BF16)


def _qkv(x2, seq_len, gain, mod, row0, w_qkv, tm=1024, tn=512):
    rows = x2.shape[0]
    nb, tps = _row_tiling(seq_len, tm)
    nq = (N_HEADS * HEAD_DIM) // tn
    kv_cols = 2 * N_KV * HEAD_DIM
    assert kv_cols == tn
    return pl.pallas_call(
        functools.partial(_qkv_kernel, nb=nb),
        grid=(rows // tm, nq),
        in_specs=[
            pl.BlockSpec((tm, D_MODEL), lambda i, j: (i, 0)),
            pl.BlockSpec((1, D_MODEL), lambda i, j: (0, 0)),
            _mod_spec(SCALE1, row0, nb, tps), _mod_spec(SHIFT1, row0, nb, tps),
            pl.BlockSpec((D_MODEL, tn), lambda i, j: (0, j)),
            pl.BlockSpec((D_MODEL, kv_cols), lambda i, j: (0, nq)),
        ],
        out_specs=[
            pl.BlockSpec((tm, tn), lambda i, j: (i, j)),
            pl.BlockSpec((tm, kv_cols), lambda i, j: (i, 0)),
        ],
        out_shape=[
            jax.ShapeDtypeStruct((rows, N_HEADS * HEAD_DIM), BF16),
            jax.ShapeDtypeStruct((rows, kv_cols), F32),
        ],
        scratch_shapes=[pltpu.VMEM((tm, D_MODEL), BF16)],
        compiler_params=_params(("arbitrary", "arbitrary")),
        name="qkv",
    )(x2, gain, mod, mod, w_qkv, w_qkv)


PAIRS_PER_KV = GROUP // 2
QROWS = PAIRS_PER_KV * CHUNK
BAND = WINDOW + CHUNK
ATT_G = 8
KVW = N_KV * HEAD_DIM


def _attn_kernel(slope_ref, sink_ref, q_ref, *refs, band, steps_per_seq):
    if band:
        kp2_ref, kp1_ref, kc_ref, vp2_ref, vp1_ref, vc_ref = refs[:6]
        refs = refs[6:]
    else:
        ck_ref, kn_ref, cv_ref, vn_ref = refs[:4]
        refs = refs[4:]
    o_ref, bias_scr, ones_scr, kl_scr, kr_scr, vl_scr, vr_scr = refs
    step = pl.program_id(0)

    @pl.when(step == 0)
    def _():
        row = lax.broadcasted_iota(jnp.int32, (QROWS, 2 * KPAD), 0)
        col = lax.broadcasted_iota(jnp.int32, (QROWS, 2 * KPAD), 1)
        key = jnp.bitwise_and(col, KPAD - 1)
        dist = jnp.abs(WINDOW + jnp.bitwise_and(row, CHUNK - 1) - key).astype(F32)
        for kv in range(N_KV):
            slope = jnp.zeros((QROWS, 2 * KPAD), F32)
            sink = jnp.zeros((QROWS, 2 * KPAD), F32)
            for pr in range(PAIRS_PER_KV):
                in_pair = jnp.logical_and(row >= pr * CHUNK, row < (pr + 1) * CHUNK)
                in_a = jnp.logical_and(in_pair, col < KPAD)
                in_b = jnp.logical_and(in_pair, col >= KPAD)
                head = kv * GROUP + pr * 2
                slope = jnp.where(in_a, slope_ref[head], jnp.where(in_b, slope_ref[head + 1], slope))
                sink = jnp.where(in_a, sink_ref[head], jnp.where(in_b, sink_ref[head + 1], sink))
            bias_scr[kv] = jnp.where(key < BAND, -(slope * dist), jnp.where(key == BAND, sink, NEG))
        orow = lax.broadcasted_iota(jnp.int32, (2 * KPAD, PAIR), 0)
        ocol = lax.broadcasted_iota(jnp.int32, (2 * KPAD, PAIR), 1)
        in_first = ocol < HEAD_DIM
        ones_scr[...] = jnp.where(orow < KPAD, jnp.where(in_first, 1.0, 0.0),
                                  jnp.where(in_first, 0.0, 1.0)).astype(BF16)

    lane64 = lax.broadcasted_iota(jnp.int32, (CHUNK, PAIR), 1) < HEAD_DIM

    def prep(src_ref, r_src, col0, r_dst, l_scr, r_scr):
        for lb in range(N_KV // 2):
            blk = src_ref[r_src:r_src + CHUNK, col0 + lb * PAIR:col0 + (lb + 1) * PAIR]
            ev = jnp.where(lane64, blk, 0.0)
            od = jnp.where(lane64, 0.0, blk)
            ds = slice(r_dst, r_dst + CHUNK)
            l_scr[2 * lb, ds, :] = ev.astype(BF16)
            r_scr[2 * lb, ds, :] = pltpu.roll(ev, HEAD_DIM, axis=1).astype(BF16)
            r_scr[2 * lb + 1, ds, :] = od.astype(BF16)
            l_scr[2 * lb + 1, ds, :] = pltpu.roll(od, HEAD_DIM, axis=1).astype(BF16)

    if band:
        prep(kp2_ref, 0, 0, 0, kl_scr, kr_scr)
        prep(kp1_ref, 0, 0, CHUNK, kl_scr, kr_scr)
        prep(vp2_ref, 0, 0, 0, vl_scr, vr_scr)
        prep(vp1_ref, 0, 0, CHUNK, vl_scr, vr_scr)
        for g in range(ATT_G):
            prep(kc_ref, g * CHUNK, 0, WINDOW + g * CHUNK, kl_scr, kr_scr)
            prep(vc_ref, g * CHUNK, 0, WINDOW + g * CHUNK, vl_scr, vr_scr)
    else:
        for g in range(ATT_G):
            for h in range(WINDOW // CHUNK):
                prep(ck_ref, g * WINDOW + h * CHUNK, 0, g * BAND + h * CHUNK, kl_scr, kr_scr)
                prep(cv_ref, g * WINDOW + h * CHUNK, 0, g * BAND + h * CHUNK, vl_scr, vr_scr)
            prep(kn_ref, g * CHUNK, 0, g * BAND + WINDOW, kl_scr, kr_scr)
            prep(vn_ref, g * CHUNK, 0, g * BAND + WINDOW, vl_scr, vr_scr)

    zpad = jnp.zeros((KPAD - BAND, PAIR), BF16)
    keyrow = jnp.bitwise_and(lax.broadcasted_iota(jnp.int32, (1, 2 * KPAD), 1), KPAD - 1)
    n0 = (step % steps_per_seq) * ATT_G

    def chunk(c, carry, masked):
        r0 = pl.multiple_of(c * CHUNK, CHUNK)
        k0 = r0 if band else pl.multiple_of(c * BAND, CHUNK)
        ks = pl.ds(k0, BAND)
        qrows = pl.ds(r0, CHUNK)
        if masked:
            maskrow = jnp.where(keyrow >= WINDOW - (n0 + c) * CHUNK, 0.0, NEG)
        for kv in range(N_KV):
            qs = jnp.concatenate(
                [q_ref[qrows, kv * GROUP * HEAD_DIM + pr * PAIR: kv * GROUP * HEAD_DIM + (pr + 1) * PAIR]
                 for pr in range(PAIRS_PER_KV)], axis=0)
            kbd = jnp.concatenate([kl_scr[kv, ks, :], zpad, kr_scr[kv, ks, :], zpad], axis=0)
            vbd = jnp.concatenate([vl_scr[kv, ks, :], zpad, vr_scr[kv, ks, :], zpad], axis=0)
            s = lax.dot_general(qs, kbd, (((1,), (1,)), ((), ())), preferred_element_type=F32)
            s = s + bias_scr[kv]
            if masked:
                s = s + maskrow
            s_a = s[:, :KPAD]
            s_b = s[:, KPAD:]
            m_a = jnp.max(s_a, axis=-1, keepdims=True)
            m_b = jnp.max(s_b, axis=-1, keepdims=True)
            p = jnp.concatenate([jnp.exp(s_a - m_a), jnp.exp(s_b - m_b)], axis=1).astype(BF16)
            ol = jnp.dot(p, jnp.concatenate([vbd, ones_scr[...]], axis=1), preferred_element_type=F32)
            o = ol[:, :PAIR] / ol[:, PAIR:]
            for pr in range(PAIRS_PER_KV):
                c0 = kv * GROUP * HEAD_DIM + pr * PAIR
                o_ref[qrows, c0:c0 + PAIR] = o[pr * CHUNK:(pr + 1) * CHUNK, :].astype(BF16)
        return carry

    n_lead = WINDOW // CHUNK if band else 0
    if n_lead:
        lax.fori_loop(0, n_lead, functools.partial(chunk, masked=True), 0, unroll=2)
    lax.fori_loop(n_lead, ATT_G, functools.partial(chunk, masked=False), 0, unroll=2)


def _attention(q, kv, cache_k, cache_v, slopes, sinks, seq_len):
    rows = q.shape[0]
    tq = ATT_G * CHUNK
    band = cache_k is None
    smem = pl.BlockSpec(memory_space=pltpu.SMEM)
    if band:
        assert seq_len % tq == 0
        sps = seq_len // tq
        cps = seq_len // CHUNK

        def prev(d, col):
            return pl.BlockSpec((CHUNK, KVW), lambda i: (jnp.maximum((i % sps) * ATT_G - d, 0) + (i // sps) * cps, col))

        kv_specs = [prev(2, 0), prev(1, 0), pl.BlockSpec((tq, KVW), lambda i: (i, 0)),
                    prev(2, 1), prev(1, 1), pl.BlockSpec((tq, KVW), lambda i: (i, 1))]
        kv_args = [kv] * 6
        key_rows = WINDOW + tq
    else:
        assert seq_len == CHUNK
        sps = 1
        kv_specs = [pl.BlockSpec((ATT_G * WINDOW, KVW), lambda i: (i, 0)), pl.BlockSpec((tq, KVW), lambda i: (i, 0)),
                    pl.BlockSpec((ATT_G * WINDOW, KVW), lambda i: (i, 0)), pl.BlockSpec((tq, KVW), lambda i: (i, 1))]
        kv_args = [cache_k, kv, cache_v, kv]
        key_rows = ATT_G * BAND
    side = pltpu.VMEM((N_KV, key_rows, PAIR), BF16)
    return pl.pallas_call(
        functools.partial(_attn_kernel, band=band, steps_per_seq=sps),
        grid=(rows // tq,),
        in_specs=[smem, smem, pl.BlockSpec((tq, N_HEADS * HEAD_DIM), lambda i: (i, 0))] + kv_specs,
        out_specs=pl.BlockSpec((tq, N_HEADS * HEAD_DIM), lambda i: (i, 0)),
        out_shape=jax.ShapeDtypeStruct((rows, N_HEADS * HEAD_DIM), BF16),
        scratch_shapes=[
            pltpu.VMEM((N_KV, QROWS, 2 * KPAD), F32),
            pltpu.VMEM((2 * KPAD, PAIR), BF16),
            side, side, side, side,
        ],
        compiler_params=_params(("arbitrary",)),
        name="attention",
    )(slopes, sinks, q, *kv_args)


def _wo_kernel(a_ref, w_ref, x_ref, gate_ref, o_ref, *, nb):
    out = jnp.dot(a_ref[...], w_ref[...], preferred_element_type=F32)
    rows = a_ref.shape[0] // nb
    for s in range(nb):
        sl = slice(s * rows, (s + 1) * rows)
        o_ref[sl, :] = x_ref[sl, :] + gate_ref[s] * out[sl, :]


def _wo(attn, x2, seq_len, mod, row0, w_o, tm=1024, tn=1024):
    rows = x2.shape[0]
    nb, tps = _row_tiling(seq_len, tm)
    return pl.pallas_call(
        functools.partial(_wo_kernel, nb=nb),
        grid=(rows // tm, D_MODEL // tn),
        in_specs=[
            pl.BlockSpec((tm, N_HEADS * HEAD_DIM), lambda i, j: (i, 0)),
            pl.BlockSpec((N_HEADS * HEAD_DIM, tn), lambda i, j: (0, j)),
            pl.BlockSpec((tm, tn), lambda i, j: (i, j)),
            _mod_spec(GATE1, row0, nb, tps, width=tn),
        ],
        out_specs=pl.BlockSpec((tm, tn), lambda i, j: (i, j)),
        out_shape=jax.ShapeDtypeStruct((rows, D_MODEL), F32),
        compiler_params=_params(("arbitrary", "arbitrary")),
        name="wo",
    )(attn, w_o, x2, mod)


def _trunk(x, mods, row0, state_conv, cache_k, cache_v, wts, late):
    batch, seq_len, _ = x.shape
    rows = batch * seq_len
    x2 = x.reshape(rows, D_MODEL)
    prompt = state_conv is None
    row = lambda v: v.reshape(1, -1)
    assert seq_len >= CONV_W - 1 and seq_len >= WINDOW // 2

    g = _pw1_glu(x2, seq_len, row(wts["norm_mix_g"][0]), mods[0], row0, wts["w_pw1"], row(wts["b_pw1"][0]))
    if prompt:
        prev = jnp.zeros((batch, HALO, D_MODEL), F32)
    else:
        prev = jnp.pad(state_conv[0], ((0, 0), (HALO - (CONV_W - 1), 0), (0, 0)))
    conv_new = g.reshape(batch, seq_len, D_MODEL)[:, -(CONV_W - 1):][None]
    casts = ()
    if late is None:
        casts = (wts["w_ffn_in"].reshape(DEPTH * D_MODEL, 2 * D_FF), wts["w_ffn_out"].reshape(DEPTH * D_FF, D_MODEL),
                 wts["w_qkv"][0], wts["w_o"][0])
    x2, cast_out = _conv_pw2(g, prev, x2, seq_len, mods[0], row0, wts["w_dw"][0], row(wts["b_dw"][0]),
                             row(wts["ln_g"][0]), row(wts["ln_b"][0]), wts["w_pw2"], row(wts["b_pw2"][0]), casts)
    if late is None:
        late = dict(w_ffn_in=cast_out[0].reshape(DEPTH, D_MODEL, 2 * D_FF),
                    w_ffn_out=cast_out[1].reshape(DEPTH, D_FF, D_MODEL), w_qkv=cast_out[2], w_o=cast_out[3])
    x2 = _ffn(x2, seq_len, row(wts["norm_ffn_g"][0]), mods[0], row0, late["w_ffn_in"], late["w_ffn_out"], 0,
              row(wts["final_norm_g"]), final_norm=False)

    q, kv = _qkv(x2, seq_len, row(wts["norm_mix_g"][1]), mods[1], row0, late["w_qkv"])
    kv3 = kv.reshape(batch, seq_len, 2 * KVW)
    if prompt:
        attn = _attention(q, kv, None, None, wts["slopes"], wts["sinks"], seq_len)
        k_new = kv3[:, -WINDOW:, :KVW]
        v_new = kv3[:, -WINDOW:, KVW:]
    else:
        ck = cache_k[0].reshape(batch * WINDOW, KVW)
        cv = cache_v[0].reshape(batch * WINDOW, KVW)
        attn = _attention(q, kv, ck, cv, wts["slopes"], wts["sinks"], seq_len)
        keep = WINDOW - seq_len
        k_new = jnp.concatenate([cache_k[0].reshape(batch, WINDOW, KVW)[:, -keep:], kv3[:, :, :KVW]], axis=1)
        v_new = jnp.concatenate([cache_v[0].reshape(batch, WINDOW, KVW)[:, -keep:], kv3[:, :, KVW:]], axis=1)
    x2 = _wo(attn, x2, seq_len, mods[1], row0, late["w_o"])
    y2 = _ffn(x2, seq_len, row(wts["norm_ffn_g"][1]), mods[1], row0, late["w_ffn_in"], late["w_ffn_out"], 1,
              row(wts["final_norm_g"]), final_norm=True)

    shape_kv = (1, batch, WINDOW, N_KV, HEAD_DIM)
    return (y2.reshape(batch, seq_len, D_MODEL), conv_new,
            k_new.reshape(shape_kv), v_new.reshape(shape_kv)), late


def kernel(x_prompt, x_sample, c_prompt, c_sample, state_conv, cache_k, cache_v, norm_mix_g, norm_ffn_g,
           w_ada, b_ada, w_pw1, b_pw1, w_dw, b_dw, ln_g, ln_b, w_pw2, b_pw2, w_qkv, w_o, sinks,
           w_ffn_in, w_ffn_out, final_norm_g):
    bp, bs = c_prompt.shape[0], c_sample.shape[0]
    b_pad = -(-(bs + bp) // 8) * 8
    c_all = jnp.concatenate([c_sample, c_prompt, jnp.zeros((b_pad - bp - bs, D_MODEL), F32)], axis=0)
    mod = _adaln(c_all, w_ada, b_ada)
    mods = [mod[l].reshape(b_pad, 1, 6 * D_MODEL) for l in range(DEPTH)]

    heads = jnp.arange(1, N_HEADS + 1, dtype=F32)
    wts = dict(
        norm_mix_g=norm_mix_g, norm_ffn_g=norm_ffn_g, final_norm_g=final_norm_g,
        w_pw1=w_pw1[0].astype(BF16), b_pw1=b_pw1, w_dw=w_dw, b_dw=b_dw, ln_g=ln_g, ln_b=ln_b,
        w_pw2=w_pw2[0].astype(BF16), b_pw2=b_pw2, w_qkv=w_qkv, w_o=w_o, w_ffn_in=w_ffn_in, w_ffn_out=w_ffn_out,
        slopes=jnp.exp2(-8.0 * heads / N_HEADS), sinks=sinks[0].astype(F32),
    )
    (y_p, conv_p, k_p, v_p), late = _trunk(x_prompt, mods, bs, None, None, None, wts, None)
    (y_s, conv_s, k_s, v_s), _ = _trunk(x_sample, mods, 0, state_conv, cache_k, cache_v, wts, late)
    return (y_p, y_s, conv_p, conv_s, k_p, v_p, k_s, v_s)
```

```python
import functools

import jax
import jax.numpy as jnp
from jax import lax
from jax.experimental import pallas as pl
from jax.experimental.pallas import tpu as pltpu

F32 = jnp.float32
BF16 = jnp.bfloat16

D_MODEL = 2048
DEPTH = 2
CHUNK = 64
CONV_W = 31
N_HEADS = 32
N_KV = 4
HEAD_DIM = 64
GROUP = N_HEADS // N_KV
WINDOW = 128
D_FF = 5632
EPS = 1e-6
NEG = -1e30

VMEM_LIMIT_BYTES = 60 * 1024 * 1024
LANES = 128
HALO = 32
PAIR = 2 * HEAD_DIM
KPAD = 256

SHIFT1, SCALE1, GATE1, SHIFT2, SCALE2, GATE2 = range(6)


def _params(sem):
    return pltpu.CompilerParams(dimension_semantics=sem, vmem_limit_bytes=VMEM_LIMIT_BYTES)


def _silu(v):
    return v * jax.nn.sigmoid(v)


FILL_ROWS = 128


def _fill_rows(h_ref, x_ref, gain_ref, scale_ref, shift_ref, nb, start, n_rows):
    rows = x_ref.shape[0] // nb
    blk = min(rows, FILL_ROWS)
    for u in range(n_rows // blk):
        at = start + u * blk
        s = 0 if nb == 1 else at // rows
        rs = pl.ds(pl.multiple_of(at, blk), blk)
        x = x_ref[rs, :]
        y = x * lax.rsqrt(jnp.mean(x * x, axis=-1, keepdims=True) + EPS)
        mul = gain_ref[...] * (1.0 + scale_ref[s])
        h_ref[rs, :] = (y * mul + shift_ref[s]).astype(BF16)


def _fill_h(h_ref, x_ref, gain_ref, scale_ref, shift_ref, nb):
    def body(r, carry):
        _fill_rows(h_ref, x_ref, gain_ref, scale_ref, shift_ref, nb, r * FILL_ROWS, FILL_ROWS)
        return carry

    lax.fori_loop(0, x_ref.shape[0] // FILL_ROWS, body, 0)


def _by_parity(i, h_a, h_b, body):
    @pl.when(i % 2 == 0)
    def _():
        body(h_a, h_b)

    @pl.when(i % 2 == 1)
    def _():
        body(h_b, h_a)


def _adaln_kernel(c_ref, w_ref, b_ref, o_ref):
    c = c_ref[...]
    a = _silu(c).astype(BF16)
    o_ref[0] = jnp.dot(a, w_ref[0].astype(BF16), preferred_element_type=F32) + b_ref[0]


def _adaln(c_all, w_ada, b_ada, tn=1024):
    bp = c_all.shape[0]
    n = w_ada.shape[-1]
    return pl.pallas_call(
        _adaln_kernel,
        grid=(DEPTH, n // tn),
        in_specs=[
            pl.BlockSpec((bp, D_MODEL), lambda l, j: (0, 0)),
            pl.BlockSpec((1, D_MODEL, tn), lambda l, j: (l, 0, j)),
            pl.BlockSpec((1, 1, tn), lambda l, j: (l, 0, j)),
        ],
        out_specs=pl.BlockSpec((1, bp, tn), lambda l, j: (l, 0, j)),
        out_shape=jax.ShapeDtypeStruct((DEPTH, bp, n), F32),
        compiler_params=_params(("arbitrary", "arbitrary")),
        name="adaln",
    )(c_all, w_ada, b_ada.reshape(DEPTH, 1, n))


def _row_tiling(seq_len, tm):
    if tm >= seq_len:
        assert tm % seq_len == 0
        return tm // seq_len, 1
    assert seq_len % tm == 0
    return 1, seq_len // tm


def _mod_spec(which, row0, nb, tps, width=D_MODEL, tile=lambda i, j: i):
    assert row0 % nb == 0
    per = D_MODEL // width
    if per == 1:
        return pl.BlockSpec((nb, 1, width), lambda i, j: (row0 // nb + tile(i, j) // tps, 0, which))
    return pl.BlockSpec((nb, 1, width), lambda i, j: (row0 // nb + tile(i, j) // tps, 0, which * per + j))


def _pw1_glu_kernel(x_ref, gain_ref, scale_ref, shift_ref, wa_ref, wb_ref, ba_ref, bb_ref,
                    g_ref, h_scr, *, nb):
    @pl.when(pl.program_id(1) == 0)
    def _():
        _fill_h(h_scr, x_ref, gain_ref, scale_ref, shift_ref, nb)

    h = h_scr[...]
    a = jnp.dot(h, wa_ref[...].astype(BF16), preferred_element_type=F32) + ba_ref[...]
    b = jnp.dot(h, wb_ref[...].astype(BF16), preferred_element_type=F32) + bb_ref[...]
    g_ref[...] = a * jax.nn.sigmoid(b)


def _pw1_glu(x2, seq_len, gain, mod, row0, w1, b1, tm=1024, tn=512):
    rows = x2.shape[0]
    nb, tps = _row_tiling(seq_len, tm)
    nj = D_MODEL // tn
    return pl.pallas_call(
        functools.partial(_pw1_glu_kernel, nb=nb),
        grid=(rows // tm, nj),
        in_specs=[
            pl.BlockSpec((tm, D_MODEL), lambda i, j: (i, 0)),
            pl.BlockSpec((1, D_MODEL), lambda i, j: (0, 0)),
            _mod_spec(SCALE1, row0, nb, tps), _mod_spec(SHIFT1, row0, nb, tps),
            pl.BlockSpec((D_MODEL, tn), lambda i, j: (0, j)),
            pl.BlockSpec((D_MODEL, tn), lambda i, j: (0, nj + j)),
            pl.BlockSpec((1, tn), lambda i, j: (0, j)),
            pl.BlockSpec((1, tn), lambda i, j: (0, nj + j)),
        ],
        out_specs=pl.BlockSpec((tm, tn), lambda i, j: (i, j)),
        out_shape=jax.ShapeDtypeStruct((rows, D_MODEL), F32),
        scratch_shapes=[pltpu.VMEM((tm, D_MODEL), BF16)],
        compiler_params=_params(("arbitrary", "arbitrary")),
        name="pw1_glu",
    )(x2, gain, mod, mod, w1, w1, b1, b1)


CONV_ROWS = 64
LN_ROWS = 128
NCHUNK = D_MODEL // LANES


def _conv_pw2_kernel(g_ref, halo_ref, prev_ref, x_ref, gate_ref, wdw_ref, bdw_ref, lng_ref, lnb_ref,
                     w2_ref, b2_ref, *rest, nb, tps, n_cast):
    cast_src = rest[:n_cast]
    o_ref = rest[n_cast]
    cast_dst = rest[n_cast + 1:2 * n_cast + 1]
    win_scr, y_scr, ybf_scr = rest[2 * n_cast + 1:]
    i = pl.program_id(0)
    j = pl.program_id(1)
    nj = pl.num_programs(1)
    tm = g_ref.shape[0]
    tt = tm // nb
    ti = jnp.minimum(i, pl.num_programs(0) - 2)
    cur = i % 2
    chunks_per_step = w2_ref.shape[1] // LANES

    for src, dst in zip(cast_src, cast_dst):
        dst[...] = src[...].astype(BF16)

    drain = i == pl.num_programs(0) - 1

    @pl.when(jnp.logical_and(i == 0, j == 0))
    def _():
        ybf_scr[1] = jnp.zeros(ybf_scr.shape[1:], BF16)

    @pl.when(jnp.logical_and(j == 0, jnp.logical_not(drain)))
    def _():
        def put_history(src_of):
            for s in range(nb):
                for c in range(NCHUNK):
                    win_scr[s * NCHUNK + c, 0:HALO, :] = src_of(s, c)

        if tps == 1:
            put_history(lambda s, c: prev_ref[s, :, c * LANES:(c + 1) * LANES])
        else:
            first = (ti % tps) == 0

            @pl.when(first)
            def _():
                put_history(lambda s, c: prev_ref[s, :, c * LANES:(c + 1) * LANES])

            @pl.when(jnp.logical_not(first))
            def _():
                put_history(lambda s, c: halo_ref[:, c * LANES:(c + 1) * LANES])
        for s in range(nb):
            for c in range(NCHUNK):
                win_scr[s * NCHUNK + c, HALO:HALO + tt, :] = g_ref[s * tt:(s + 1) * tt, c * LANES:(c + 1) * LANES]

    lead = HALO - (CONV_W - 1)

    def matmul_prev():
        out = jnp.dot(ybf_scr[1 - cur], w2_ref[...], preferred_element_type=F32) + b2_ref[...]
        for s in range(nb):
            sl = slice(s * tt, (s + 1) * tt)
            o_ref[sl, :] = x_ref[sl, :] + gate_ref[s] * out[sl, :]

    def conv_and_matmul(jj):
        for cc in range(chunks_per_step):
            c = jj * chunks_per_step + cc
            for s in range(nb):
                for rb in range(tt // CONV_ROWS):
                    acc = jnp.zeros((CONV_ROWS, LANES), F32) + bdw_ref[c]
                    for k in range(CONV_W):
                        r0 = rb * CONV_ROWS + k + lead
                        acc = acc + win_scr[s * NCHUNK + c, r0:r0 + CONV_ROWS, :] * wdw_ref[c, k:k + 1, :]
                    y_scr[c, s * tt + rb * CONV_ROWS:s * tt + (rb + 1) * CONV_ROWS, :] = acc
        matmul_prev()

    for jj in range(NCHUNK // chunks_per_step):
        pl.when(jnp.logical_and(j == jj, jnp.logical_not(drain)))(functools.partial(conv_and_matmul, jj))
    pl.when(drain)(matmul_prev)

    @pl.when(jnp.logical_and(j == nj - 1, jnp.logical_not(drain)))
    def _():
        def ln_block(r, carry):
            rs = pl.ds(pl.multiple_of(r * LN_ROWS, LN_ROWS), LN_ROWS)
            tot = y_scr[0, rs, :]
            for c in range(1, NCHUNK):
                tot = tot + y_scr[c, rs, :]
            mu = jnp.sum(tot, axis=-1, keepdims=True) * (1.0 / D_MODEL)
            sq = jnp.zeros((LN_ROWS, LANES), F32)
            for c in range(NCHUNK):
                yc = y_scr[c, rs, :] - mu
                sq = sq + yc * yc
            rstd = lax.rsqrt(jnp.sum(sq, axis=-1, keepdims=True) * (1.0 / D_MODEL) + EPS)
            for c in range(NCHUNK):
                cs = slice(c * LANES, (c + 1) * LANES)
                z = (y_scr[c, rs, :] - mu) * rstd * lng_ref[:, cs] + lnb_ref[:, cs]
                ybf_scr[cur, rs, cs] = _silu(z).astype(BF16)
            return carry

        lax.fori_loop(0, tm // LN_ROWS, ln_block, 0)


def _conv_pw2(g2, prev, x2, seq_len, mod, row0, w_dw, b_dw, ln_g, ln_b, w2, b2, casts=(), tm=512, tn=512):
    rows = g2.shape[0]
    nb, tps = _row_tiling(seq_len, tm)
    tt = tm // nb
    hb = tm // HALO
    nj = D_MODEL // tn
    n_tiles = rows // tm
    n_steps = n_tiles * nj
    cast_specs = []
    for a in casts:
        slab = a.shape[0] // n_steps
        assert a.shape[0] % n_steps == 0 and slab % 16 == 0, a.shape
        cast_specs.append(pl.BlockSpec((slab, a.shape[1]), lambda i, j: (jnp.minimum(i * nj + j, n_steps - 1), 0)))
    front = lambda i: jnp.minimum(i, n_tiles - 1)
    back = lambda i: jnp.maximum(i - 1, 0)
    assert row0 % nb == 0
    outs = pl.pallas_call(
        functools.partial(_conv_pw2_kernel, nb=nb, tps=tps, n_cast=len(casts)),
        grid=(n_tiles + 1, nj),
        in_specs=[
            pl.BlockSpec((tm, D_MODEL), lambda i, j: (front(i), 0)),
            pl.BlockSpec((HALO, D_MODEL), lambda i, j: (jnp.maximum(front(i) * hb - 1, 0), 0)),
            pl.BlockSpec((nb, HALO, D_MODEL), lambda i, j: (front(i) // tps, 0, 0)),
            pl.BlockSpec((tm, tn), lambda i, j: (back(i), j)),
            pl.BlockSpec((nb, 1, tn), lambda i, j: (row0 // nb + back(i) // tps, 0, GATE1 * nj + j)),
            pl.BlockSpec((NCHUNK, CONV_W, LANES), lambda i, j: (0, 0, 0)),
            pl.BlockSpec((NCHUNK, 1, LANES), lambda i, j: (0, 0, 0)),
            pl.BlockSpec((1, D_MODEL), lambda i, j: (0, 0)),
            pl.BlockSpec((1, D_MODEL), lambda i, j: (0, 0)),
            pl.BlockSpec((D_MODEL, tn), lambda i, j: (0, j)),
            pl.BlockSpec((1, tn), lambda i, j: (0, j)),
        ] + cast_specs,
        out_specs=[pl.BlockSpec((tm, tn), lambda i, j: (back(i), j * jnp.minimum(i, 1)))] + cast_specs,
        out_shape=[jax.ShapeDtypeStruct((rows, D_MODEL), F32)]
        + [jax.ShapeDtypeStruct(a.shape, BF16) for a in casts],
        scratch_shapes=[
            pltpu.VMEM((nb * NCHUNK, HALO + tt, LANES), F32),
            pltpu.VMEM((NCHUNK, tm, LANES), F32),
            pltpu.VMEM((2, tm, D_MODEL), BF16),
        ],
        compiler_params=_params(("arbitrary", "arbitrary")),
        name="conv_pw2",
    )(g2, g2, prev, x2, mod,
      w_dw.reshape(CONV_W, NCHUNK, LANES).transpose(1, 0, 2), b_dw.reshape(NCHUNK, 1, LANES),
      ln_g, ln_b, w2, b2, *casts)
    return outs[0], list(outs[1:])


FFN_SUB = 512


def _ffn_kernel(x_ref, gain_ref, scale_ref, shift_ref, gate_ref, wa_ref, wb_ref, wo_ref, fin_ref,
                o_ref, h_a, h_b, *, nb, final_norm):
    i = pl.program_id(0)
    f = pl.program_id(1)
    tm = x_ref.shape[0]
    rows = tm // nb

    @pl.when(f == 0)
    def _():
        @pl.when(i == 0)
        def _():
            _fill_h(h_a, x_ref, gain_ref, scale_ref, shift_ref, nb)

        o_ref[...] = x_ref[...]

    slab = jnp.clip(f - 1, 0, tm // FILL_ROWS - 1)

    def body(h_cur, h_next):
        _fill_rows(h_next, x_ref, gain_ref, scale_ref, shift_ref, nb, slab * FILL_ROWS, FILL_ROWS)
        for m in range(tm // FFN_SUB):
            h = h_cur[m * FFN_SUB:(m + 1) * FFN_SUB, :]
            a = jnp.dot(h, wa_ref[...], preferred_element_type=F32)
            b = jnp.dot(h, wb_ref[...], preferred_element_type=F32)
            act = (_silu(a) * b).astype(BF16)
            res = jnp.dot(act, wo_ref[...], preferred_element_type=F32)
            piece = min(rows, FFN_SUB)
            for u in range(FFN_SUB // piece):
                lo = m * FFN_SUB + u * piece
                o_ref[lo:lo + piece, :] += gate_ref[lo // rows] * res[u * piece:(u + 1) * piece, :]

    _by_parity(i, h_a, h_b, body)

    if final_norm:
        @pl.when(f == pl.num_programs(1) - 1)
        def _():
            def body(r, carry):
                rs = pl.ds(pl.multiple_of(r * FILL_ROWS, FILL_ROWS), FILL_ROWS)
                y = o_ref[rs, :]
                o_ref[rs, :] = y * lax.rsqrt(jnp.mean(y * y, axis=-1, keepdims=True) + EPS) * fin_ref[...]
                return carry

            lax.fori_loop(0, tm // FILL_ROWS, body, 0)


def _ffn(x2, seq_len, gain, mod, row0, w_in, w_out, layer, fin_gain, final_norm, tm=1024, tf=512):
    rows = x2.shape[0]
    nb, tps = _row_tiling(seq_len, tm)
    nf = D_FF // tf
    n_tiles = rows // tm
    assert nf - 1 >= tm // FILL_ROWS
    vec_spec = pl.BlockSpec((1, D_MODEL), lambda i, f: (0, 0))
    x_tile = lambda i, f: jnp.minimum(i + jnp.minimum(f, 1), n_tiles - 1)
    return pl.pallas_call(
        functools.partial(_ffn_kernel, nb=nb, final_norm=final_norm),
        grid=(n_tiles, nf),
        in_specs=[
            pl.BlockSpec((tm, D_MODEL), lambda i, f: (x_tile(i, f), 0), pipeline_mode=pl.Buffered(1)),
            vec_spec,
            _mod_spec(SCALE2, row0, nb, tps, tile=x_tile), _mod_spec(SHIFT2, row0, nb, tps, tile=x_tile),
            _mod_spec(GATE2, row0, nb, tps),
            pl.BlockSpec((None, D_MODEL, tf), lambda i, f: (layer, 0, f)),
            pl.BlockSpec((None, D_MODEL, tf), lambda i, f: (layer, 0, nf + f)),
            pl.BlockSpec((None, tf, D_MODEL), lambda i, f: (layer, f, 0)),
            vec_spec,
        ],
        out_specs=pl.BlockSpec((tm, D_MODEL), lambda i, f: (i, 0)),
        out_shape=jax.ShapeDtypeStruct((rows, D_MODEL), F32),
        scratch_shapes=[pltpu.VMEM((tm, D_MODEL), BF16), pltpu.VMEM((tm, D_MODEL), BF16)],
        compiler_params=_params(("arbitrary", "arbitrary")),
        name="ffn",
    )(x2, gain, mod, mod, mod, w_in, w_in, w_out, fin_gain)


def _qkv_kernel(x_ref, gain_ref, scale_ref, shift_ref, wq_ref, wkv_ref, q_ref, kv_ref, h_scr, *, nb):
    @pl.when(pl.program_id(1) == 0)
    def _():
        _fill_h(h_scr, x_ref, gain_ref, scale_ref, shift_ref, nb)
        kv_ref[...] = jnp.dot(h_scr[...], wkv_ref[...], preferred_element_type=F32)

    q = jnp.dot(h_scr[...], wq_ref[...], preferred_element_type=F32)
    q_ref[...] = (q * (HEAD_DIM ** -0.5)).astype(BF16)


def _qkv(x2, seq_len, gain, mod, row0, w_qkv, tm=1024, tn=512):
    rows = x2.shape[0]
    nb, tps = _row_tiling(seq_len, tm)
    nq = (N_HEADS * HEAD_DIM) // tn
    kv_cols = 2 * N_KV * HEAD_DIM
    assert kv_cols == tn
    return pl.pallas_call(
        functools.partial(_qkv_kernel, nb=nb),
        grid=(rows // tm, nq),
        in_specs=[
            pl.BlockSpec((tm, D_MODEL), lambda i, j: (i, 0)),
            pl.BlockSpec((1, D_MODEL), lambda i, j: (0, 0)),
            _mod_spec(SCALE1, row0, nb, tps), _mod_spec(SHIFT1, row0, nb, tps),
            pl.BlockSpec((D_MODEL, tn), lambda i, j: (0, j)),
            pl.BlockSpec((D_MODEL, kv_cols), lambda i, j: (0, nq)),
        ],
        out_specs=[
            pl.BlockSpec((tm, tn), lambda i, j: (i, j)),
            pl.BlockSpec((tm, kv_cols), lambda i, j: (i, 0)),
        ],
        out_shape=[
            jax.ShapeDtypeStruct((rows, N_HEADS * HEAD_DIM), BF16),
            jax.ShapeDtypeStruct((rows, kv_cols), F32),
        ],
        scratch_shapes=[pltpu.VMEM((tm, D_MODEL), BF16)],
        compiler_params=_params(("arbitrary", "arbitrary")),
        name="qkv",
    )(x2, gain, mod, mod, w_qkv, w_qkv)


PAIRS_PER_KV = GROUP // 2
QROWS = PAIRS_PER_KV * CHUNK
BAND = WINDOW + CHUNK
ATT_G = 8
KVW = N_KV * HEAD_DIM


def _attn_kernel(slope_ref, sink_ref, q_ref, *refs, band, steps_per_seq):
    if band:
        kp2_ref, kp1_ref, kc_ref, vp2_ref, vp1_ref, vc_ref = refs[:6]
        refs = refs[6:]
    else:
        ck_ref, kn_ref, cv_ref, vn_ref = refs[:4]
        refs = refs[4:]
    o_ref, bias_scr, ones_scr, kl_scr, kr_scr, vl_scr, vr_scr = refs
    step = pl.program_id(0)

    @pl.when(step == 0)
    def _():
        row = lax.broadcasted_iota(jnp.int32, (QROWS, 2 * KPAD), 0)
        col = lax.broadcasted_iota(jnp.int32, (QROWS, 2 * KPAD), 1)
        key = jnp.bitwise_and(col, KPAD - 1)
        dist = jnp.abs(WINDOW + jnp.bitwise_and(row, CHUNK - 1) - key).astype(F32)
        for kv in range(N_KV):
            slope = jnp.zeros((QROWS, 2 * KPAD), F32)
            sink = jnp.zeros((QROWS, 2 * KPAD), F32)
            for pr in range(PAIRS_PER_KV):
                in_pair = jnp.logical_and(row >= pr * CHUNK, row < (pr + 1) * CHUNK)
                in_a = jnp.logical_and(in_pair, col < KPAD)
                in_b = jnp.logical_and(in_pair, col >= KPAD)
                head = kv * GROUP + pr * 2
                slope = jnp.where(in_a, slope_ref[head], jnp.where(in_b, slope_ref[head + 1], slope))
                sink = jnp.where(in_a, sink_ref[head], jnp.where(in_b, sink_ref[head + 1], sink))
            bias_scr[kv] = jnp.where(key < BAND, -(slope * dist), jnp.where(key == BAND, sink, NEG))
        orow = lax.broadcasted_iota(jnp.int32, (2 * KPAD, PAIR), 0)
        ocol = lax.broadcasted_iota(jnp.int32, (2 * KPAD, PAIR), 1)
        in_first = ocol < HEAD_DIM
        ones_scr[...] = jnp.where(orow < KPAD, jnp.where(in_first, 1.0, 0.0),
                                  jnp.where(in_first, 0.0, 1.0)).astype(BF16)

    lane64 = lax.broadcasted_iota(jnp.int32, (CHUNK, PAIR), 1) < HEAD_DIM

    def prep(src_ref, r_src, col0, r_dst, l_scr, r_scr):
        for lb in range(N_KV // 2):
            blk = src_ref[r_src:r_src + CHUNK, col0 + lb * PAIR:col0 + (lb + 1) * PAIR]
            ev = jnp.where(lane64, blk, 0.0)
            od = jnp.where(lane64, 0.0, blk)
            ds = slice(r_dst, r_dst + CHUNK)
            l_scr[2 * lb, ds, :] = ev.astype(BF16)
            r_scr[2 * lb, ds, :] = pltpu.roll(ev, HEAD_DIM, axis=1).astype(BF16)
            r_scr[2 * lb + 1, ds, :] = od.astype(BF16)
            l_scr[2 * lb + 1, ds, :] = pltpu.roll(od, HEAD_DIM, axis=1).astype(BF16)

    if band:
        prep(kp2_ref, 0, 0, 0, kl_scr, kr_scr)
        prep(kp1_ref, 0, 0, CHUNK, kl_scr, kr_scr)
        prep(vp2_ref, 0, 0, 0, vl_scr, vr_scr)
        prep(vp1_ref, 0, 0, CHUNK, vl_scr, vr_scr)
        for g in range(ATT_G):
            prep(kc_ref, g * CHUNK, 0, WINDOW + g * CHUNK, kl_scr, kr_scr)
            prep(vc_ref, g * CHUNK, 0, WINDOW + g * CHUNK, vl_scr, vr_scr)
    else:
        for g in range(ATT_G):
            for h in range(WINDOW // CHUNK):
                prep(ck_ref, g * WINDOW + h * CHUNK, 0, g * BAND + h * CHUNK, kl_scr, kr_scr)
                prep(cv_ref, g * WINDOW + h * CHUNK, 0, g * BAND + h * CHUNK, vl_scr, vr_scr)
            prep(kn_ref, g * CHUNK, 0, g * BAND + WINDOW, kl_scr, kr_scr)
            prep(vn_ref, g * CHUNK, 0, g * BAND + WINDOW, vl_scr, vr_scr)

    zpad = jnp.zeros((KPAD - BAND, PAIR), BF16)
    keyrow = jnp.bitwise_and(lax.broadcasted_iota(jnp.int32, (1, 2 * KPAD), 1), KPAD - 1)
    n0 = (step % steps_per_seq) * ATT_G

    def chunk(c, carry, masked):
        r0 = pl.multiple_of(c * CHUNK, CHUNK)
        k0 = r0 if band else pl.multiple_of(c * BAND, CHUNK)
        ks = pl.ds(k0, BAND)
        qrows = pl.ds(r0, CHUNK)
        if masked:
            maskrow = jnp.where(keyrow >= WINDOW - (n0 + c) * CHUNK, 0.0, NEG)
        for kv in range(N_KV):
            qs = jnp.concatenate(
                [q_ref[qrows, kv * GROUP * HEAD_DIM + pr * PAIR: kv * GROUP * HEAD_DIM + (pr + 1) * PAIR]
                 for pr in range(PAIRS_PER_KV)], axis=0)
            kbd = jnp.concatenate([kl_scr[kv, ks, :], zpad, kr_scr[kv, ks, :], zpad], axis=0)
            vbd = jnp.concatenate([vl_scr[kv, ks, :], zpad, vr_scr[kv, ks, :], zpad], axis=0)
            s = lax.dot_general(qs, kbd, (((1,), (1,)), ((), ())), preferred_element_type=F32)
            s = s + bias_scr[kv]
            if masked:
                s = s + maskrow
            s_a = s[:, :KPAD]
            s_b = s[:, KPAD:]
            m_a = jnp.max(s_a, axis=-1, keepdims=True)
            m_b = jnp.max(s_b, axis=-1, keepdims=True)
            p = jnp.concatenate([jnp.exp(s_a - m_a), jnp.exp(s_b - m_b)], axis=1).astype(BF16)
            ol = jnp.dot(p, jnp.concatenate([vbd, ones_scr[...]], axis=1), preferred_element_type=F32)
            o = ol[:, :PAIR] / ol[:, PAIR:]
            for pr in range(PAIRS_PER_KV):
                c0 = kv * GROUP * HEAD_DIM + pr * PAIR
                o_ref[qrows, c0:c0 + PAIR] = o[pr * CHUNK:(pr + 1) * CHUNK, :].astype(BF16)
        return carry

    n_lead = WINDOW // CHUNK if band else 0
    if n_lead:
        lax.fori_loop(0, n_lead, functools.partial(chunk, masked=True), 0, unroll=2)
    lax.fori_loop(n_lead, ATT_G, functools.partial(chunk, masked=False), 0, unroll=2)


def _attention(q, kv, cache_k, cache_v, slopes, sinks, seq_len):
    rows = q.shape[0]
    tq = ATT_G * CHUNK
    band = cache_k is None
    smem = pl.BlockSpec(memory_space=pltpu.SMEM)
    if band:
        assert seq_len % tq == 0
        sps = seq_len // tq
        cps = seq_len // CHUNK

        def prev(d, col):
            return pl.BlockSpec((CHUNK, KVW), lambda i: (jnp.maximum((i % sps) * ATT_G - d, 0) + (i // sps) * cps, col))

        kv_specs = [prev(2, 0), prev(1, 0), pl.BlockSpec((tq, KVW), lambda i: (i, 0)),
                    prev(2, 1), prev(1, 1), pl.BlockSpec((tq, KVW), lambda i: (i, 1))]
        kv_args = [kv] * 6
        key_rows = WINDOW + tq
    else:
        assert seq_len == CHUNK
        sps = 1
        kv_specs = [pl.BlockSpec((ATT_G * WINDOW, KVW), lambda i: (i, 0)), pl.BlockSpec((tq, KVW), lambda i: (i, 0)),
                    pl.BlockSpec((ATT_G * WINDOW, KVW), lambda i: (i, 0)), pl.BlockSpec((tq, KVW), lambda i: (i, 1))]
        kv_args = [cache_k, kv, cache_v, kv]
        key_rows = ATT_G * BAND
    side = pltpu.VMEM((N_KV, key_rows, PAIR), BF16)
    return pl.pallas_call(
        functools.partial(_attn_kernel, band=band, steps_per_seq=sps),
        grid=(rows // tq,),
        in_specs=[smem, smem, pl.BlockSpec((tq, N_HEADS * HEAD_DIM), lambda i: (i, 0))] + kv_specs,
        out_specs=pl.BlockSpec((tq, N_HEADS * HEAD_DIM), lambda i: (i, 0)),
        out_shape=jax.ShapeDtypeStruct((rows, N_HEADS * HEAD_DIM), BF16),
        scratch_shapes=[
            pltpu.VMEM((N_KV, QROWS, 2 * KPAD), F32),
            pltpu.VMEM((2 * KPAD, PAIR), BF16),
            side, side, side, side,
        ],
        compiler_params=_params(("arbitrary",)),
        name="attention",
    )(slopes, sinks, q, *kv_args)


def _wo_kernel(a_ref, w_ref, x_ref, gate_ref, o_ref, *, nb):
    out = jnp.dot(a_ref[...], w_ref[...], preferred_element_type=F32)
    rows = a_ref.shape[0] // nb
    for s in range(nb):
        sl = slice(s * rows, (s + 1) * rows)
        o_ref[sl, :] = x_ref[sl, :] + gate_ref[s] * out[sl, :]


def _wo(attn, x2, seq_len, mod, row0, w_o, tm=1024, tn=1024):
    rows = x2.shape[0]
    nb, tps = _row_tiling(seq_len, tm)
    return pl.pallas_call(
        functools.partial(_wo_kernel, nb=nb),
        grid=(rows // tm, D_MODEL // tn),
        in_specs=[
            pl.BlockSpec((tm, N_HEADS * HEAD_DIM), lambda i, j: (i, 0)),
            pl.BlockSpec((N_HEADS * HEAD_DIM, tn), lambda i, j: (0, j)),
            pl.BlockSpec((tm, tn), lambda i, j: (i, j)),
            _mod_spec(GATE1, row0, nb, tps, width=tn),
        ],
        out_specs=pl.BlockSpec((tm, tn), lambda i, j: (i, j)),
        out_shape=jax.ShapeDtypeStruct((rows, D_MODEL), F32),
        compiler_params=_params(("arbitrary", "arbitrary")),
        name="wo",
    )(attn, w_o, x2, mod)


def _trunk(x, mods, row0, state_conv, cache_k, cache_v, wts, late):
    batch, seq_len, _ = x.shape
    rows = batch * seq_len
    x2 = x.reshape(rows, D_MODEL)
    prompt = state_conv is None
    row = lambda v: v.reshape(1, -1)
    assert seq_len >= CONV_W - 1 and seq_len >= WINDOW // 2

    g = _pw1_glu(x2, seq_len, row(wts["norm_mix_g"][0]), mods[0], row0, wts["w_pw1"], row(wts["b_pw1"][0]))
    if prompt:
        prev = jnp.zeros((batch, HALO, D_MODEL), F32)
    else:
        prev = jnp.pad(state_conv[0], ((0, 0), (HALO - (CONV_W - 1), 0), (0, 0)))
    conv_new = g.reshape(batch, seq_len, D_MODEL)[:, -(CONV_W - 1):][None]
    casts = ()
    if late is None:
        casts = (wts["w_ffn_in"].reshape(DEPTH * D_MODEL, 2 * D_FF), wts["w_ffn_out"].reshape(DEPTH * D_FF, D_MODEL),
                 wts["w_qkv"][0], wts["w_o"][0])
    x2, cast_out = _conv_pw2(g, prev, x2, seq_len, mods[0], row0, wts["w_dw"][0], row(wts["b_dw"][0]),
                             row(wts["ln_g"][0]), row(wts["ln_b"][0]), wts["w_pw2"], row(wts["b_pw2"][0]), casts)
    if late is None:
        late = dict(w_ffn_in=cast_out[0].reshape(DEPTH, D_MODEL, 2 * D_FF),
                    w_ffn_out=cast_out[1].reshape(DEPTH, D_FF, D_MODEL), w_qkv=cast_out[2], w_o=cast_out[3])
    x2 = _ffn(x2, seq_len, row(wts["norm_ffn_g"][0]), mods[0], row0, late["w_ffn_in"], late["w_ffn_out"], 0,
              row(wts["final_norm_g"]), final_norm=False)

    q, kv = _qkv(x2, seq_len, row(wts["norm_mix_g"][1]), mods[1], row0, late["w_qkv"])
    kv3 = kv.reshape(batch, seq_len, 2 * KVW)
    if prompt:
        attn = _attention(q, kv, None, None, wts["slopes"], wts["sinks"], seq_len)
        k_new = kv3[:, -WINDOW:, :KVW]
        v_new = kv3[:, -WINDOW:, KVW:]
    else:
        ck = cache_k[0].reshape(batch * WINDOW, KVW)
        cv = cache_v[0].reshape(batch * WINDOW, KVW)
        attn = _attention(q, kv, ck, cv, wts["slopes"], wts["sinks"], seq_len)
        keep = WINDOW - seq_len
        k_new = jnp.concatenate([cache_k[0].reshape(batch, WINDOW, KVW)[:, -keep:], kv3[:, :, :KVW]], axis=1)
        v_new = jnp.concatenate([cache_v[0].reshape(batch, WINDOW, KVW)[:, -keep:], kv3[:, :, KVW:]], axis=1)
    x2 = _wo(attn, x2, seq_len, mods[1], row0, late["w_o"])
    y2 = _ffn(x2, seq_len, row(wts["norm_ffn_g"][1]), mods[1], row0, late["w_ffn_in"], late["w_ffn_out"], 1,
              row(wts["final_norm_g"]), final_norm=True)

    shape_kv = (1, batch, WINDOW, N_KV, HEAD_DIM)
    return (y2.reshape(batch, seq_len, D_MODEL), conv_new,
            k_new.reshape(shape_kv), v_new.reshape(shape_kv)), late


def kernel(x_prompt, x_sample, c_prompt, c_sample, state_conv, cache_k, cache_v, norm_mix_g, norm_ffn_g,
           w_ada, b_ada, w_pw1, b_pw1, w_dw, b_dw, ln_g, ln_b, w_pw2, b_pw2, w_qkv, w_o, sinks,
           w_ffn_in, w_ffn_out, final_norm_g):
    bp, bs = c_prompt.shape[0], c_sample.shape[0]
    b_pad = -(-(bs + bp) // 8) * 8
    c_all = jnp.concatenate([c_sample, c_prompt, jnp.zeros((b_pad - bp - bs, D_MODEL), F32)], axis=0)
    mod = _adaln(c_all, w_ada, b_ada)
    mods = [mod[l].reshape(b_pad, 1, 6 * D_MODEL) for l in range(DEPTH)]

    heads = jnp.arange(1, N_HEADS + 1, dtype=F32)
    wts = dict(
        norm_mix_g=norm_mix_g, norm_ffn_g=norm_ffn_g, final_norm_g=final_norm_g,
        w_pw1=w_pw1[0], b_pw1=b_pw1, w_dw=w_dw, b_dw=b_dw, ln_g=ln_g, ln_b=ln_b,
        w_pw2=w_pw2[0].astype(BF16), b_pw2=b_pw2, w_qkv=w_qkv, w_o=w_o, w_ffn_in=w_ffn_in, w_ffn_out=w_ffn_out,
        slopes=jnp.exp2(-8.0 * heads / N_HEADS), sinks=sinks[0].astype(F32),
    )
    (y_p, conv_p, k_p, v_p), late = _trunk(x_prompt, mods, bs, None, None, None, wts, None)
    (y_s, conv_s, k_s, v_s), _ = _trunk(x_sample, mods, 0, state_conv, cache_k, cache_v, wts, late)
    return (y_p, y_s, conv_p, conv_s, k_p, v_p, k_s, v_s)
```
